```python
import math, functools
import jax, jax.numpy as jnp
from jax import lax
import numpy as np

D_MODEL = 1024
BATCH = 8
SEQ = 2048
DEPTH = 4
DEC_BATCH = 32
DEC_SEQ = 1
PAST_LEN = 8192
PAGE_SIZE = 128

MIX_WIDTH = D_MODEL
ATT_WIDTH = MIX_WIDTH // 2
N_ATT_HEADS = 8
HEAD_DIM = ATT_WIDTH // N_ATT_HEADS
SSM_WIDTH = MIX_WIDTH - ATT_WIDTH
SSM_GROUP = 16
N_SSM_GROUPS = SSM_WIDTH // SSM_GROUP
SSM_STATE = 64
D_FF = 4 * D_MODEL
PLE_DIM = 256
Q_BLOCK = 128
RMS_EPS = 1e-6
ATT_BIAS_INIT = -8.0

kernel_name = 'hymba_style_sb_attn_s5_step'


def rms_norm(x, g):
    xf = x.astype(jnp.float32)
    y = xf * lax.rsqrt(jnp.mean(xf * xf, axis=-1, keepdims=True) + RMS_EPS)
    return (y * g.astype(jnp.float32)).astype(x.dtype)


def stick_breaking(q, k, v, q_pos, k_pos, bias):
    z = (jnp.einsum('bqhd,bkhd->bhqk', q.astype(jnp.float32), k.astype(jnp.float32)) * (HEAD_DIM ** -0.5)
         + bias.astype(jnp.float32)[None, :, None, None])
    mask = k_pos[None, :] < q_pos[:, None]
    log_keep = jnp.where(mask, jax.nn.log_sigmoid(-z), 0.0)
    between = lax.cumsum(log_keep, axis=3, reverse=True) - log_keep
    w = jnp.where(mask, jnp.exp(jax.nn.log_sigmoid(z) + between), 0.0)
    return jnp.einsum('bhqk,bkhd->bqhd', w, v.astype(jnp.float32))


def attend_prompt(q, k, v, bias):
    b, s = q.shape[0], q.shape[1]
    n_blk = s // Q_BLOCK
    qb = q.reshape(b, n_blk, Q_BLOCK, N_ATT_HEADS, HEAD_DIM).transpose(1, 0, 2, 3, 4)
    k_pos = jnp.arange(s)

    def one_block(args):
        q_blk, blk = args
        q_pos = blk * Q_BLOCK + jnp.arange(Q_BLOCK)
        return stick_breaking(q_blk, k, v, q_pos, k_pos, bias)

    out = lax.map(one_block, (qb, jnp.arange(n_blk)))
    return out.transpose(1, 0, 2, 3, 4).reshape(b, s, N_ATT_HEADS, HEAD_DIM)


def attend_sample(q, k, v, bias, past_k, past_v):
    past, t = past_k.shape[1], q.shape[1]
    k_all = jnp.concatenate([past_k.astype(k.dtype), k], axis=1)
    v_all = jnp.concatenate([past_v.astype(v.dtype), v], axis=1)
    k_pos = jnp.arange(past + t)
    q_pos = past + jnp.arange(t)
    return stick_breaking(q, k_all, v_all, q_pos, k_pos, bias)


def complex_affine_combine(e1, e2):
    a1r, a1i, b1r, b1i = e1
    a2r, a2i, b2r, b2i = e2
    ar = a2r * a1r - a2i * a1i
    ai = a2r * a1i + a2i * a1r
    br = a2r * b1r - a2i * b1i + b2r
    bi = a2r * b1i + a2i * b1r + b2i
    return ar, ai, br, bi


def ssm_mix(u, h0_re, h0_im, a_re, a_im, log_dt, b_re, b_im, c_re, c_im, d_skip, w_glu):
    f32 = jnp.float32
    bsz, L = u.shape[0], u.shape[1]
    uf = u.astype(f32).reshape(bsz, L, N_SSM_GROUPS, SSM_GROUP)
    a_re = a_re.astype(f32)
    a_im = a_im.astype(f32)
    dt = jnp.exp(log_dt.astype(f32))[:, None]
    mag = jnp.exp(a_re * dt)
    ab_re = mag * jnp.cos(a_im * dt)
    ab_im = mag * jnp.sin(a_im * dt)
    den = a_re * a_re + a_im * a_im
    f_re = ((ab_re - 1.0) * a_re + ab_im * a_im) / den
    f_im = (ab_im * a_re - (ab_re - 1.0) * a_im) / den
    br = b_re.astype(f32)
    bi = b_im.astype(f32)
    bb_re = f_re[..., None] * br - f_im[..., None] * bi
    bb_im = f_re[..., None] * bi + f_im[..., None] * br
    bu_re = jnp.einsum('blgc,gpc->blgp', uf, bb_re)
    bu_im = jnp.einsum('blgc,gpc->blgp', uf, bb_im)
    h0r = h0_re.astype(f32)
    h0i = h0_im.astype(f32)
    bu_re = bu_re.at[:, 0].add(ab_re * h0r - ab_im * h0i)
    bu_im = bu_im.at[:, 0].add(ab_re * h0i + ab_im * h0r)
    shape = bu_re.shape
    elems = (jnp.broadcast_to(ab_re, shape), jnp.broadcast_to(ab_im, shape), bu_re, bu_im)
    _, _, h_re, h_im = lax.associative_scan(complex_affine_combine, elems, axis=1)
    y = (jnp.einsum('gcp,blgp->blgc', c_re.astype(f32), h_re)
         - jnp.einsum('gcp,blgp->blgc', c_im.astype(f32), h_im)
         + d_skip.astype(f32) * uf)
    y = jax.nn.gelu(y).reshape(bsz, L, SSM_WIDTH)
    y = y * jax.nn.sigmoid(y @ w_glu.astype(f32))
    return y.astype(u.dtype), h_re[:, -1], h_im[:, -1]


def decoder_layer(h, p_l, attend, h0_re, h0_im, w_in, w_out, g_mix, g_att_out, g_ssm_out, att_bias,
                  a_re, a_im, log_dt, b_re, b_im, c_re, c_im, d_skip, w_glu,
                  g_ffn, w_up, w_down, w_ple, w_ple_gate, g_ple):
    bsz, L = h.shape[0], h.shape[1]
    proj = rms_norm(h, g_mix) @ w_in
    q = proj[..., :ATT_WIDTH].reshape(bsz, L, N_ATT_HEADS, HEAD_DIM)
    k = proj[..., ATT_WIDTH:2 * ATT_WIDTH].reshape(bsz, L, N_ATT_HEADS, HEAD_DIM)
    v = proj[..., 2 * ATT_WIDTH:3 * ATT_WIDTH].reshape(bsz, L, N_ATT_HEADS, HEAD_DIM)
    u = proj[..., 3 * ATT_WIDTH:]
    att = attend(q, k, v, att_bias).reshape(bsz, L, ATT_WIDTH).astype(h.dtype)
    ssm, hr, hi = ssm_mix(u, h0_re, h0_im, a_re, a_im, log_dt, b_re, b_im, c_re, c_im, d_skip, w_glu)
    mixed = jnp.concatenate([rms_norm(att, g_att_out), rms_norm(ssm, g_ssm_out)], axis=-1) @ w_out
    h = h + mixed
    f = rms_norm(h, g_ffn) @ w_up
    h = h + jnp.square(jax.nn.relu(f)) @ w_down
    gate = jax.nn.sigmoid(rms_norm(h, g_ple) @ w_ple_gate)
    h = h + (p_l @ w_ple) * gate
    return h, k, v, hr, hi


def setup_inputs(seed: int = 0) -> dict:
    key = jax.random.key(seed)
    ks = jax.random.split(key, 32)
    f32 = jnp.float32

    def nrm(k, shape, scale):
        return jax.random.normal(k, shape, f32) * scale

    n_pages = PAST_LEN // PAGE_SIZE
    n_used = DEC_BATCH * n_pages
    n_pool = n_used + max(1, n_used // 4)
    G, P, C = N_SSM_GROUPS, SSM_STATE, SSM_GROUP
    n_idx = jnp.arange(P, dtype=f32)
    proj_out = 3 * ATT_WIDTH + SSM_WIDTH
    page_table = jax.random.permutation(ks[8], n_pool)[:n_used].reshape(DEC_BATCH, n_pages).astype(jnp.int32)
    return {
        'x_prompt': nrm(ks[0], (BATCH, SEQ, D_MODEL), 1.0),
        'x_sample': nrm(ks[1], (DEC_BATCH, DEC_SEQ, D_MODEL), 1.0),
        'p_prompt': nrm(ks[2], (DEPTH, BATCH, SEQ, PLE_DIM), 1.0),
        'p_sample': nrm(ks[3], (DEPTH, DEC_BATCH, DEC_SEQ, PLE_DIM), 1.0),
        'cache_k': nrm(ks[4], (DEPTH, n_pool, PAGE_SIZE, N_ATT_HEADS, HEAD_DIM), 1.0),
        'cache_v': nrm(ks[5], (DEPTH, n_pool, PAGE_SIZE, N_ATT_HEADS, HEAD_DIM), 1.0),
        'state_ssm_re': nrm(ks[6], (DEPTH, DEC_BATCH, G, P), 0.5),
        'state_ssm_im': nrm(ks[7], (DEPTH, DEC_BATCH, G, P), 0.5),
        'page_table': page_table,
        'w_in': nrm(ks[9], (DEPTH, D_MODEL, proj_out), D_MODEL ** -0.5),
        'w_out': nrm(ks[10], (DEPTH, MIX_WIDTH, D_MODEL), MIX_WIDTH ** -0.5),
        'g_mix': 1.0 + nrm(ks[11], (DEPTH, D_MODEL), 0.05),
        'g_att_out': 1.0 + nrm(ks[12], (DEPTH, ATT_WIDTH), 0.05),
        'g_ssm_out': 1.0 + nrm(ks[13], (DEPTH, SSM_WIDTH), 0.05),
        'att_bias': ATT_BIAS_INIT + nrm(ks[30], (DEPTH, N_ATT_HEADS), 0.5),
        'ssm_a_re': -0.5 + nrm(ks[14], (DEPTH, G, P), 0.01),
        'ssm_a_im': jnp.pi * n_idx + nrm(ks[15], (DEPTH, G, P), 0.01),
        'ssm_log_dt': jax.random.uniform(ks[16], (DEPTH, G), f32, math.log(1e-3), math.log(1e-1)),
        'ssm_b_re': nrm(ks[17], (DEPTH, G, P, C), 0.7 * C ** -0.5),
        'ssm_b_im': nrm(ks[18], (DEPTH, G, P, C), 0.7 * C ** -0.5),
        'ssm_c_re': nrm(ks[19], (DEPTH, G, C, P), 0.7 * P ** -0.5),
        'ssm_c_im': nrm(ks[20], (DEPTH, G, C, P), 0.7 * P ** -0.5),
        'ssm_d': nrm(ks[21], (DEPTH, G, C), 0.5),
        'w_glu': nrm(ks[22], (DEPTH, SSM_WIDTH, SSM_WIDTH), SSM_WIDTH ** -0.5),
        'g_ffn': 1.0 + nrm(ks[23], (DEPTH, D_MODEL), 0.05),
        'w_up': nrm(ks[24], (DEPTH, D_MODEL, D_FF), D_MODEL ** -0.5),
        'w_down': nrm(ks[25], (DEPTH, D_FF, D_MODEL), D_FF ** -0.5),
        'w_ple': nrm(ks[26], (DEPTH, PLE_DIM, D_MODEL), PLE_DIM ** -0.5),
        'w_ple_gate': nrm(ks[27], (DEPTH, D_MODEL, D_MODEL), D_MODEL ** -0.5),
        'g_ple': 1.0 + nrm(ks[28], (DEPTH, D_MODEL), 0.05),
        'g_final': 1.0 + nrm(ks[29], (D_MODEL,), 0.05),
    }


def reference(x_prompt, x_sample, p_prompt, p_sample, cache_k, cache_v, state_ssm_re, state_ssm_im,
              page_table, w_in, w_out, g_mix, g_att_out, g_ssm_out, att_bias, ssm_a_re, ssm_a_im, ssm_log_dt,
              ssm_b_re, ssm_b_im, ssm_c_re, ssm_c_im, ssm_d, w_glu, g_ffn, w_up, w_down,
              w_ple, w_ple_gate, g_ple, g_final):
    n_pages = PAST_LEN // PAGE_SIZE
    dec_b = x_sample.shape[0]
    zeros_state = jnp.zeros((x_prompt.shape[0], N_SSM_GROUPS, SSM_STATE), jnp.float32)
    hp, hs = x_prompt, x_sample
    kp_l, vp_l, rp_l, ip_l, ks_l, vs_l, rs_l, is_l = [], [], [], [], [], [], [], []
    for l in range(DEPTH):
        lw = (w_in[l], w_out[l], g_mix[l], g_att_out[l], g_ssm_out[l], att_bias[l],
              ssm_a_re[l], ssm_a_im[l], ssm_log_dt[l], ssm_b_re[l], ssm_b_im[l],
              ssm_c_re[l], ssm_c_im[l], ssm_d[l], w_glu[l],
              g_ffn[l], w_up[l], w_down[l], w_ple[l], w_ple_gate[l], g_ple[l])
        hp, kp, vp, rp, ip = decoder_layer(hp, p_prompt[l], attend_prompt, zeros_state, zeros_state, *lw)
        past_k = cache_k[l][page_table].reshape(dec_b, n_pages * PAGE_SIZE, N_ATT_HEADS, HEAD_DIM)
        past_v = cache_v[l][page_table].reshape(dec_b, n_pages * PAGE_SIZE, N_ATT_HEADS, HEAD_DIM)
        attend = functools.partial(attend_sample, past_k=past_k, past_v=past_v)
        hs, ks_, vs_, rs_, is_ = decoder_layer(hs, p_sample[l], attend, state_ssm_re[l], state_ssm_im[l], *lw)
        kp_l.append(kp); vp_l.append(vp); rp_l.append(rp); ip_l.append(ip)
        ks_l.append(ks_); vs_l.append(vs_); rs_l.append(rs_); is_l.append(is_)
    y_prompt = rms_norm(hp, g_final)
    y_sample = rms_norm(hs, g_final)
    return (y_prompt, y_sample,
            jnp.stack(kp_l), jnp.stack(vp_l), jnp.stack(rp_l), jnp.stack(ip_l),
            jnp.stack(ks_l), jnp.stack(vs_l), jnp.stack(rs_l), jnp.stack(is_l))
```

```python
import functools

import jax
import jax.numpy as jnp
from jax import lax
from jax.experimental import pallas as pl
from jax.experimental.pallas import tpu as pltpu

RMS_EPS = 1e-6
LANES = 128
SUBLANES = 8
VMEM_LIMIT = 56 * 1024 * 1024
BF16 = jnp.bfloat16
F32 = jnp.float32
_NT = (((1,), (1,)), ((), ()))


def _mm(a, w):
    if w.dtype == BF16:
        return jnp.dot(a.astype(BF16), w, preferred_element_type=F32)
    return jnp.dot(a, w, preferred_element_type=F32, precision=lax.Precision.HIGHEST)


def _rms(x, g):
    return x * lax.rsqrt(jnp.mean(x * x, axis=-1, keepdims=True) + RMS_EPS) * g


def _const_spec(shape):
    nd = len(shape)
    return pl.BlockSpec(shape, lambda *_: (0,) * nd, pipeline_mode=pl.Buffered(1))


def _params(*sem):
    return pltpu.CompilerParams(dimension_semantics=sem, vmem_limit_bytes=VMEM_LIMIT)


def _ssm_prep_kernel(are_ref, aim_ref, ldt_ref, bre_ref, bim_ref, cre_ref, cim_ref,
                     bbd_ref, cbd_ref, ab_ref, *, c_shift, p_shift):
    a_re = are_ref[0]
    a_im = aim_ref[0]
    dt = jnp.exp(ldt_ref[0])
    mag = jnp.exp(a_re * dt)
    ab_re = mag * jnp.cos(a_im * dt)
    ab_im = mag * jnp.sin(a_im * dt)
    den = a_re * a_re + a_im * a_im
    f_re = ((ab_re - 1.0) * a_re + ab_im * a_im) / den
    f_im = (ab_im * a_re - (ab_re - 1.0) * a_im) / den
    cb, sp = bre_ref.shape[1], bre_ref.shape[2]
    ch0 = pl.program_id(1) * cb
    same_bc = ((lax.broadcasted_iota(jnp.int32, (cb, sp), 0) + ch0) >> c_shift) == \
              (lax.broadcasted_iota(jnp.int32, (cb, sp), 1) >> p_shift)
    br = bre_ref[0]
    bi = bim_ref[0]
    bbd_ref[0, :, :sp] = jnp.where(same_bc, f_re * br - f_im * bi, 0.0)
    bbd_ref[0, :, sp:] = jnp.where(same_bc, f_re * bi + f_im * br, 0.0)
    same_cb = (lax.broadcasted_iota(jnp.int32, (sp, cb), 0) >> p_shift) == \
              ((lax.broadcasted_iota(jnp.int32, (sp, cb), 1) + ch0) >> c_shift)
    cbd_ref[0, :sp, :] = jnp.where(same_cb, cre_ref[0], 0.0)
    cbd_ref[0, sp:, :] = jnp.where(same_cb, -cim_ref[0], 0.0)
    ab_ref[0, :, :sp] = jnp.broadcast_to(ab_re, (SUBLANES, sp))
    ab_ref[0, :, sp:] = jnp.broadcast_to(ab_im, (SUBLANES, sp))


def _ssm_prep(a_re, a_im, log_dt, b_re, b_im, c_re, c_im):
    depth, g, p = a_re.shape
    c = b_re.shape[-1]
    sw, sp = g * c, g * p
    assert c & (c - 1) == 0 and p & (p - 1) == 0 and sw % LANES == 0
    flat = lambda x: x.reshape(depth, 1, sp)
    ldt = jnp.repeat(log_dt, p, axis=1).reshape(depth, 1, sp)
    tile_b = lambda b: jnp.tile(b.transpose(0, 1, 3, 2).reshape(depth, sw, p), (1, 1, g))
    tile_c = lambda x: jnp.tile(x.transpose(0, 3, 1, 2).reshape(depth, p, sw), (1, g, 1))
    vec = pl.BlockSpec((1, 1, sp), lambda l, j: (l, 0, 0))
    bspec = pl.BlockSpec((1, LANES, sp), lambda l, j: (l, j, 0))
    cspec = pl.BlockSpec((1, sp, LANES), lambda l, j: (l, 0, j))
    return pl.pallas_call(
        functools.partial(_ssm_prep_kernel, c_shift=c.bit_length() - 1, p_shift=p.bit_length() - 1),
        grid=(depth, sw // LANES),
        in_specs=[vec, vec, vec, bspec, bspec, cspec, cspec],
        out_specs=[pl.BlockSpec((1, LANES, 2 * sp), lambda l, j: (l, j, 0)),
                   pl.BlockSpec((1, 2 * sp, LANES), lambda l, j: (l, 0, j)),
                   pl.BlockSpec((1, SUBLANES, 2 * sp), lambda l, j: (l, 0, 0))],
        out_shape=[jax.ShapeDtypeStruct((depth, sw, 2 * sp), F32),
                   jax.ShapeDtypeStruct((depth, 2 * sp, sw), F32),
                   jax.ShapeDtypeStruct((depth, SUBLANES, 2 * sp), F32)],
        compiler_params=_params("arbitrary", "arbitrary"),
        name="ssm_prep",
    )(flat(a_re), flat(a_im), ldt, tile_b(b_re), tile_b(b_im), tile_c(c_re), tile_c(c_im))


def _in_proj_kernel(x_ref, g_ref, wq_ref, wk_ref, wv_ref, wu_ref, q_ref, k_ref, v_ref, u_ref, *, kv_transposed):
    xn = _rms(x_ref[...], g_ref[...])
    if wq_ref.dtype == BF16:
        xn = xn.astype(BF16)
    q_ref[...] = _mm(xn, wq_ref[...])
    u_ref[...] = _mm(xn, wu_ref[...])
    if kv_transposed:
        k_ref[0] = lax.dot_general(wk_ref[...], xn, _NT, preferred_element_type=F32)
        v_ref[0] = lax.dot_general(wv_ref[...], xn, _NT, preferred_element_type=F32)
    else:
        k_ref[...] = _mm(xn, wk_ref[...])
        v_ref[...] = _mm(xn, wv_ref[...])


def _in_proj(x, g, lw, tm, prompt_batch=None):
    t, d = x.shape
    aw, sw = lw["w_q"].shape[1], lw["w_u"].shape[1]
    nt = t // tm
    row = pl.BlockSpec((tm, aw), lambda i: (i, 0))
    if prompt_batch is None:
        kv_spec, kv_shape = row, (t, aw)
        u_spec, u_shape = pl.BlockSpec((tm, sw), lambda i: (i, 0)), (t, sw)
    else:
        nlt = nt // prompt_batch
        kv_spec = pl.BlockSpec((1, aw, tm), lambda i: (i // nlt, 0, i % nlt))
        kv_shape = (prompt_batch, aw, t // prompt_batch)
        u_spec = pl.BlockSpec((tm, sw), lambda i: (i % nlt, i // nlt))
        u_shape = (t // prompt_batch, prompt_batch * sw)
    return pl.pallas_call(
        functools.partial(_in_proj_kernel, kv_transposed=prompt_batch is not None),
        grid=(nt,),
        in_specs=[pl.BlockSpec((tm, d), lambda i: (i, 0)), _const_spec((1, d)), _const_spec(lw["w_q"].shape),
                  _const_spec(lw["w_k"].shape), _const_spec(lw["w_v"].shape), _const_spec(lw["w_u"].shape)],
        out_specs=[row, kv_spec, kv_spec, u_spec],
        out_shape=[jax.ShapeDtypeStruct((t, aw), F32), jax.ShapeDtypeStruct(kv_shape, F32),
                   jax.ShapeDtypeStruct(kv_shape, F32), jax.ShapeDtypeStruct(u_shape, F32)],
        compiler_params=_params("parallel"),
        name="in_proj",
    )(x, g.reshape(1, d), lw["w_q"], lw["w_k"], lw["w_v"], lw["w_u"])


def _stick_terms(z):
    sp = jnp.maximum(z, 0.0) + jnp.log1p(jnp.exp(-jnp.abs(z)))
    return sp, z - sp


def _suffix_sums(sp, u2_ref):
    hi = sp.astype(BF16)
    lo = (sp - hi.astype(F32)).astype(BF16)
    return jnp.dot(jnp.concatenate([hi, lo], axis=1), u2_ref[...], preferred_element_type=F32)


def _suffix_matrix():
    j = lax.broadcasted_iota(jnp.int32, (2 * LANES, 2 * LANES), 0) % LANES
    s = lax.broadcasted_iota(jnp.int32, (2 * LANES, 2 * LANES), 1)
    return jnp.where((s >= LANES) | (j > s), 1.0, 0.0).astype(BF16)


def _attn_prompt_kernel(bias_ref, q_ref, k_ref, v_ref, u2_ref, o_ref, kb_ref, vb_ref, acc_ref, car_ref, *, hd):
    tq = LANES
    seq = q_ref.shape[1]
    hp = pl.program_id(1)
    for j in range(seq // tq):
        cols = slice(j * tq, (j + 1) * tq)
        kb_ref[j] = k_ref[0, :, cols].astype(BF16)
        vb_ref[j] = v_ref[0, :, cols].T.astype(BF16)
    lane = lax.broadcasted_iota(jnp.int32, (tq, LANES), 1)
    first_head = lane < hd
    causal = lax.broadcasted_iota(jnp.int32, (tq, tq), 1) < lax.broadcasted_iota(jnp.int32, (tq, tq), 0)
    scale = hd ** -0.5
    bias = [bias_ref[2 * hp], bias_ref[2 * hp + 1]]

    def tile(h, qm, j, diag):
        z = jnp.dot(qm, kb_ref[j], preferred_element_type=F32) + bias[h]
        sp, ls = _stick_terms(z)
        if diag:
            sp = jnp.where(causal, sp, 0.0)
        c = _suffix_sums(sp, u2_ref)
        if diag:
            w = jnp.where(causal, jnp.exp(ls - c[:, :tq]), 0.0)
            car_ref[h] = c[:, tq:]
            acc_ref[h] = jnp.dot(w.astype(BF16), vb_ref[j], preferred_element_type=F32)
        else:
            car = car_ref[h]
            w = jnp.exp(ls - c[:, :tq] - car)
            car_ref[h] = car + c[:, tq:]
            acc_ref[h] += jnp.dot(w.astype(BF16), vb_ref[j], preferred_element_type=F32)

    def q_block(i, _):
        rows = pl.ds(pl.multiple_of(i * tq, tq), tq)
        q = q_ref[0, rows, :] * scale
        qms = [jnp.where(first_head, q, 0.0).astype(BF16), jnp.where(first_head, 0.0, q).astype(BF16)]
        for h in range(2):
            tile(h, qms[h], i, True)

        def k_block(jj, _):
            for h in range(2):
                tile(h, qms[h], i - 1 - jj, False)
            return 0

        lax.fori_loop(0, i, k_block, 0)
        o_ref[0, rows, :] = jnp.where(first_head, acc_ref[0], acc_ref[1])
        return 0

    lax.fori_loop(0, seq // tq, q_block, 0)


def _attn_prompt(q, k_t, v_t, bias, u2, hd):
    b, seq, aw = q.shape
    assert 2 * hd == LANES and seq % LANES == 0 and aw % LANES == 0
    blk = pl.BlockSpec((1, seq, LANES), lambda bi, hp: (bi, 0, hp))
    blk_t = pl.BlockSpec((1, LANES, seq), lambda bi, hp: (bi, hp, 0))
    return pl.pallas_call(
        functools.partial(_attn_prompt_kernel, hd=hd),
        grid=(b, aw // LANES),
        in_specs=[pl.BlockSpec(memory_space=pltpu.SMEM), blk, blk_t, blk_t,
                  _const_spec((2 * LANES, 2 * LANES))],
        out_specs=blk,
        out_shape=jax.ShapeDtypeStruct((b, seq, aw), F32),
        scratch_shapes=[pltpu.VMEM((seq // LANES, LANES, LANES), BF16),
                        pltpu.VMEM((seq // LANES, LANES, LANES), BF16),
                        pltpu.VMEM((2, LANES, LANES), F32), pltpu.VMEM((2, LANES, LANES), F32)],
        compiler_params=_params("parallel", "parallel"),
        name="attn_prompt",
    )(bias, q, k_t, v_t, u2)


def _attn_decode_kernel(pt_ref, q_ref, bias_ref, k_ref, v_ref, u2_ref, o_ref, qbd_ref, acc_ref, car_ref, *, hd):
    del pt_ref
    j = pl.program_id(1)
    rows, aw = qbd_ref.shape
    own = (lax.broadcasted_iota(jnp.int32, (rows, aw), 1) // hd) == lax.broadcasted_iota(jnp.int32, (rows, aw), 0)

    @pl.when(j == 0)
    def _():
        q = jnp.broadcast_to(q_ref[0] * (hd ** -0.5), (rows, aw))
        qbd_ref[...] = jnp.where(own, q, 0.0).astype(BF16)
        acc_ref[...] = jnp.zeros_like(acc_ref)
        car_ref[...] = jnp.zeros_like(car_ref)

    z = jnp.dot(qbd_ref[...], k_ref[0, 0].astype(BF16), preferred_element_type=F32) + bias_ref[...]
    sp, ls = _stick_terms(z)
    c = _suffix_sums(sp, u2_ref)
    car = car_ref[...]
    w = jnp.exp(ls - c[:, :LANES] - car)
    car_ref[...] = car + c[:, LANES:]
    acc_ref[...] += lax.dot_general(w.astype(BF16), v_ref[0, 0].astype(BF16), _NT, preferred_element_type=F32)

    @pl.when(j == pl.num_programs(1) - 1)
    def _():
        o_ref[0] = jnp.sum(jnp.where(own, acc_ref[...], 0.0), axis=0, keepdims=True)


def _attn_decode(q, bias, cache_k, cache_v, page_table, layer, u2, hd):
    s, aw = q.shape
    n_pages = page_table.shape[1]
    page = cache_k.shape[3]
    nh = aw // hd
    rows = 2 * SUBLANES
    assert page == LANES and nh <= rows
    bias_rows = jnp.broadcast_to(jnp.pad(bias, (0, rows - nh))[:, None], (rows, LANES))
    kv_spec = pl.BlockSpec((1, 1, aw, page), lambda b, j, pt: (layer, pt[b, n_pages - 1 - j], 0, 0))
    q_spec = pl.BlockSpec((1, 1, aw), lambda b, j, pt: (b, 0, 0))
    out = pl.pallas_call(
        functools.partial(_attn_decode_kernel, hd=hd),
        grid_spec=pltpu.PrefetchScalarGridSpec(
            num_scalar_prefetch=1,
            grid=(s, n_pages),
            in_specs=[q_spec, pl.BlockSpec((rows, LANES), lambda b, j, pt: (0, 0)), kv_spec, kv_spec,
                      pl.BlockSpec((2 * LANES, 2 * LANES), lambda b, j, pt: (0, 0))],
            out_specs=q_spec,
            scratch_shapes=[pltpu.VMEM((rows, aw), BF16), pltpu.VMEM((rows, aw), F32),
                            pltpu.VMEM((rows, LANES), F32)],
        ),
        out_shape=jax.ShapeDtypeStruct((s, 1, aw), F32),
        compiler_params=_params("parallel", "arbitrary"),
        name="attn_decode",
    )(page_table, q.reshape(s, 1, aw), bias_rows, cache_k, cache_v, u2)
    return out.reshape(s, aw)


def _ssm_kernel(u_ref, bbd_ref, cbd_ref, ab_ref, d_ref, wglu_ref, h0_ref, y_ref, ht_ref, bu_ref, h_ref,
                *, rows, steps, cw):
    c = pl.program_id(0)
    sp = h_ref.shape[1] // 2

    @pl.when(c == 0)
    def _():
        h_ref[...] = h0_ref[...]

    u = u_ref[...]
    bu_ref[...] = _mm(u, bbd_ref[...])
    for cc in range(sp // cw):
        re = slice(cc * cw, (cc + 1) * cw)
        im = slice(sp + cc * cw, sp + (cc + 1) * cw)
        a_r = jnp.broadcast_to(ab_ref[0:1, re], (rows, cw))
        a_i = jnp.broadcast_to(ab_ref[0:1, im], (rows, cw))

        def step(t, carry):
            h_r, h_i = carry
            r = pl.ds(pl.multiple_of(t * rows, rows), rows)
            n_r = a_r * h_r - a_i * h_i + bu_ref[r, re]
            n_i = a_r * h_i + a_i * h_r + bu_ref[r, im]
            bu_ref[r, re] = n_r
            bu_ref[r, im] = n_i
            return n_r, n_i

        h_r, h_i = lax.fori_loop(0, steps, step, (h_ref[:, re], h_ref[:, im]), unroll=min(steps, 8))
        h_ref[:, re] = h_r
        h_ref[:, im] = h_i

    y = _mm(bu_ref[...], cbd_ref[...]) + d_ref[...] * u
    y = jax.nn.gelu(y, approximate=True)
    y_ref[...] = y * jax.nn.sigmoid(_mm(y, wglu_ref[...]))

    @pl.when(c == pl.num_programs(0) - 1)
    def _():
        ht_ref[...] = h_ref[...]


def _ssm(u, bbd, cbd, ab, d_skip, w_glu, h0, rows, steps):
    n, sw = u.shape
    sp2 = bbd.shape[1]
    chunk = rows * steps
    cw = 4 * LANES * SUBLANES // rows if rows <= 4 * SUBLANES else LANES
    cw = max(LANES, min(cw, sp2 // 2))
    blk = pl.BlockSpec((chunk, sw), lambda c: (c, 0))
    return pl.pallas_call(
        functools.partial(_ssm_kernel, rows=rows, steps=steps, cw=cw),
        grid=(n // chunk,),
        in_specs=[blk, _const_spec((sw, sp2)), _const_spec((sp2, sw)), _const_spec((SUBLANES, sp2)),
                  _const_spec((1, sw)), _const_spec((sw, sw)), _const_spec((rows, sp2))],
        out_specs=[blk, pl.BlockSpec((rows, sp2), lambda c: (0, 0))],
        out_shape=[jax.ShapeDtypeStruct((n, sw), F32), jax.ShapeDtypeStruct((rows, sp2), F32)],
        scratch_shapes=[pltpu.VMEM((chunk, sp2), F32), pltpu.VMEM((rows, sp2), F32)],
        compiler_params=_params("arbitrary"),
        name="ssm",
    )(u, bbd, cbd, ab, d_skip.reshape(1, sw), w_glu, h0)


def _tail_kernel(h_ref, att_ref, ssm_ref, p_ref, ga_ref, gs_ref, woa_ref, wos_ref, gf_ref, wup_ref, wdn_ref,
                 gp_ref, wg_ref, wp_ref, gfin_ref, o_ref, acc_ref, xn_ref, *, n_sub, final):
    c = pl.program_id(1)

    @pl.when(c == 0)
    def _():
        na = _rms(att_ref[...], ga_ref[...])
        ns = _rms(ssm_ref[...], gs_ref[...])
        h1 = h_ref[...] + _mm(na, woa_ref[...]) + _mm(ns, wos_ref[...])
        acc_ref[...] = h1
        xn_ref[...] = _rms(h1, gf_ref[...]).astype(xn_ref.dtype)

    xn = xn_ref[...]
    sub = wup_ref.shape[1] // n_sub
    ffn = None
    for s in range(n_sub):
        f = _mm(xn, wup_ref[:, s * sub:(s + 1) * sub])
        part = _mm(jnp.square(jnp.maximum(f, 0.0)), wdn_ref[s * sub:(s + 1) * sub, :])
        ffn = part if ffn is None else ffn + part
    acc_ref[...] += ffn

    @pl.when(c == pl.num_programs(1) - 1)
    def _():
        h2 = acc_ref[...]
        gate = jax.nn.sigmoid(_mm(_rms(h2, gp_ref[...]), wg_ref[...]))
        h3 = h2 + _mm(p_ref[...], wp_ref[...]) * gate
        o_ref[...] = _rms(h3, gfin_ref[...]) if final else h3


def _tail(h, att, ssm, p, lw, g_final, tm, ff_chunks, n_sub, final, time_major_batch=None):
    t, d = h.shape
    aw, sw, pd = att.shape[1], lw["w_out_s"].shape[0], p.shape[1]
    d_ff = lw["w_up"].shape[1]
    ffc = d_ff // ff_chunks
    nt = t // tm
    wdt = lw["w_up"].dtype
    if time_major_batch is None:
        ssm_spec = pl.BlockSpec((tm, sw), lambda i, c: (i, 0))
    else:
        nlt = nt // time_major_batch
        ssm_spec = pl.BlockSpec((tm, sw), lambda i, c: (i % nlt, i // nlt))
    tok = lambda w: pl.BlockSpec((tm, w), lambda i, c: (i, 0))
    if ff_chunks == 1:
        up_spec, dn_spec = _const_spec((d, d_ff)), _const_spec((d_ff, d))
    else:
        up_spec = pl.BlockSpec((d, ffc), lambda i, c: (0, c))
        dn_spec = pl.BlockSpec((ffc, d), lambda i, c: (c, 0))
    vec = lambda g: g.reshape(1, -1)
    return pl.pallas_call(
        functools.partial(_tail_kernel, n_sub=n_sub, final=final),
        grid=(nt, ff_chunks),
        in_specs=[tok(d), tok(aw), ssm_spec, tok(pd),
                  _const_spec((1, aw)), _const_spec((1, sw)), _const_spec((aw, d)), _const_spec((sw, d)),
                  _const_spec((1, d)), up_spec, dn_spec,
                  _const_spec((1, d)), _const_spec((d, d)), _const_spec((pd, d)), _const_spec((1, d))],
        out_specs=tok(d),
        out_shape=jax.ShapeDtypeStruct((t, d), F32),
        scratch_shapes=[pltpu.VMEM((tm, d), F32), pltpu.VMEM((tm, d), wdt)],
        compiler_params=_params("parallel", "arbitrary"),
        name="tail",
    )(h, att, ssm, p, vec(lw["g_att_out"]), vec(lw["g_ssm_out"]), lw["w_out_a"], lw["w_out_s"],
      vec(lw["g_ffn"]), lw["w_up"], lw["w_down"], vec(lw["g_ple"]), lw["w_ple_gate"], lw["w_ple"], vec(g_final))


PROMPT_TILE = 512
SSM_STEPS = 64
FF_SUB = 1024


def kernel(x_prompt, x_sample, p_prompt, p_sample, cache_k, cache_v, state_ssm_re, state_ssm_im, page_table, w_in, w_out, g_mix, g_att_out, g_ssm_out, att_bias, ssm_a_re, ssm_a_im, ssm_log_dt, ssm_b_re, ssm_b_im, ssm_c_re, ssm_c_im, ssm_d, w_glu, g_ffn, w_up, w_down, w_ple, w_ple_gate, g_ple, g_final):
    b, seq, d = x_prompt.shape
    sb, st = x_sample.shape[0], x_sample.shape[1]
    assert st == 1 and b % SUBLANES == 0
    depth, n_groups, n_state = ssm_a_re.shape
    n_heads = att_bias.shape[1]
    hd = cache_k.shape[-1]
    aw = n_heads * hd
    sw = n_groups * ssm_b_re.shape[-1]
    sp = n_groups * n_state
    d_ff = w_up.shape[2]
    tm = min(PROMPT_TILE, seq)
    steps = min(SSM_STEPS, seq)
    n_sub = max(1, d_ff // FF_SUB)

    bbd, cbd, ab = _ssm_prep(ssm_a_re, ssm_a_im, ssm_log_dt, ssm_b_re, ssm_b_im, ssm_c_re, ssm_c_im)
    u2 = _suffix_matrix()
    pages_t = lambda c: c.transpose(0, 1, 3, 4, 2).reshape(depth, c.shape[1], aw, c.shape[2])
    cache_k, cache_v = pages_t(cache_k), pages_t(cache_v)

    def layer_weights(l, dt, kv_transposed):
        cast = lambda w: w[l].astype(dt)
        kv = (lambda w: cast(w).T) if kv_transposed else cast
        return {"w_q": cast(w_in[:, :, :aw]), "w_k": kv(w_in[:, :, aw:2 * aw]), "w_v": kv(w_in[:, :, 2 * aw:3 * aw]),
                "w_u": cast(w_in[:, :, 3 * aw:]), "w_out_a": cast(w_out[:, :aw]), "w_out_s": cast(w_out[:, aw:]),
                "w_up": cast(w_up), "w_down": cast(w_down), "w_ple": cast(w_ple),
                "w_ple_gate": cast(w_ple_gate), "w_glu": cast(w_glu), "bbd": cast(bbd), "cbd": cast(cbd),
                "g_att_out": g_att_out[l], "g_ssm_out": g_ssm_out[l], "g_ffn": g_ffn[l], "g_ple": g_ple[l]}

    hp = x_prompt.reshape(b * seq, d)
    hs = x_sample.reshape(sb, d)
    zero_state = jnp.zeros((b, 2 * sp), F32)
    outs = {name: [] for name in ("kp", "vp", "rp", "ip", "ks", "vs", "rs", "is")}
    for l in range(depth):
        final = l == depth - 1
        lw = layer_weights(l, BF16, True)
        q, k, v, u = _in_proj(hp, g_mix[l], lw, tm, prompt_batch=b)
        att = _attn_prompt(q.reshape(b, seq, aw), k, v, att_bias[l], u2, hd)
        y, h_last = _ssm(u.reshape(seq * b, sw), lw["bbd"], lw["cbd"], ab[l], ssm_d[l].reshape(-1),
                         lw["w_glu"], zero_state, rows=b, steps=steps)
        hp = _tail(hp, att.reshape(b * seq, aw), y.reshape(seq, b * sw), p_prompt[l].reshape(b * seq, -1),
                   lw, g_final, tm, 1, n_sub, final, time_major_batch=b)
        outs["kp"].append(k.reshape(b, n_heads, hd, seq))
        outs["vp"].append(v.reshape(b, n_heads, hd, seq))
        outs["rp"].append(h_last[:, :sp].reshape(b, n_groups, n_state))
        outs["ip"].append(h_last[:, sp:].reshape(b, n_groups, n_state))

        lw = layer_weights(l, F32, False)
        q, k, v, u = _in_proj(hs, g_mix[l], lw, sb)
        att = _attn_decode(q, att_bias[l], cache_k, cache_v, page_table, l, u2, hd)
        h0 = jnp.concatenate([state_ssm_re[l].reshape(sb, sp), state_ssm_im[l].reshape(sb, sp)], axis=1)
        y, h_last = _ssm(u, lw["bbd"], lw["cbd"], ab[l], ssm_d[l].reshape(-1), lw["w_glu"], h0,
                         rows=sb, steps=1)
        hs = _tail(hs, att, y, p_sample[l].reshape(sb, -1), lw, g_final, sb, n_sub, 1, final)
        outs["ks"].append(k.reshape(sb, 1, n_heads, hd))
        outs["vs"].append(v.reshape(sb, 1, n_heads, hd))
        outs["rs"].append(h_last[:, :sp].reshape(sb, n_groups, n_state))
        outs["is"].append(h_last[:, sp:].reshape(sb, n_groups, n_state))

    stack = lambda name: jnp.stack(outs[name])
    seq_major = lambda x: x.transpose(0, 1, 4, 2, 3)
    return (hp.reshape(b, seq, d), hs.reshape(sb, st, d),
            seq_major(stack("kp")), seq_major(stack("vp")), stack("rp"), stack("ip"),
            stack("ks"), stack("vs"), stack("rs"), stack("is"))
```

```python
import functools

import jax
import jax.numpy as jnp
from jax import lax
from jax.experimental import pallas as pl
from jax.experimental.pallas import tpu as pltpu

RMS_EPS = 1e-6
LANES = 128
SUBLANES = 8
VMEM_LIMIT = 56 * 1024 * 1024
BF16 = jnp.bfloat16
F32 = jnp.float32
_NT = (((1,), (1,)), ((), ()))
LOG2E = 1.4426950408889634
ATTN_GROUP = 4
DECODE_PAGES = 8


def _mm(a, w):
    if w.dtype == BF16:
        return jnp.dot(a.astype(BF16), w, preferred_element_type=F32)
    return jnp.dot(a, w, preferred_element_type=F32, precision=lax.Precision.HIGHEST)


def _rms(x, g):
    return x * lax.rsqrt(jnp.mean(x * x, axis=-1, keepdims=True) + RMS_EPS) * g


def _const_spec(shape):
    nd = len(shape)
    return pl.BlockSpec(shape, lambda *_: (0,) * nd, pipeline_mode=pl.Buffered(1))


def _params(*sem):
    return pltpu.CompilerParams(dimension_semantics=sem, vmem_limit_bytes=VMEM_LIMIT)


def _ssm_prep_kernel(are_ref, aim_ref, ldt_ref, bre_ref, bim_ref, cre_ref, cim_ref,
                     bbd_ref, cbd_ref, ab_ref, *, c_shift, p_shift):
    a_re = are_ref[0]
    a_im = aim_ref[0]
    dt = jnp.exp(ldt_ref[0])
    mag = jnp.exp(a_re * dt)
    ab_re = mag * jnp.cos(a_im * dt)
    ab_im = mag * jnp.sin(a_im * dt)
    den = a_re * a_re + a_im * a_im
    f_re = ((ab_re - 1.0) * a_re + ab_im * a_im) / den
    f_im = (ab_im * a_re - (ab_re - 1.0) * a_im) / den
    cb, sp = bre_ref.shape[1], bre_ref.shape[2]
    ch0 = pl.program_id(1) * cb
    same_bc = ((lax.broadcasted_iota(jnp.int32, (cb, sp), 0) + ch0) >> c_shift) == \
              (lax.broadcasted_iota(jnp.int32, (cb, sp), 1) >> p_shift)
    br = bre_ref[0]
    bi = bim_ref[0]
    bbd_ref[0, :, :sp] = jnp.where(same_bc, f_re * br - f_im * bi, 0.0)
    bbd_ref[0, :, sp:] = jnp.where(same_bc, f_re * bi + f_im * br, 0.0)
    same_cb = (lax.broadcasted_iota(jnp.int32, (sp, cb), 0) >> p_shift) == \
              ((lax.broadcasted_iota(jnp.int32, (sp, cb), 1) + ch0) >> c_shift)
    cbd_ref[0, :sp, :] = jnp.where(same_cb, cre_ref[0], 0.0)
    cbd_ref[0, sp:, :] = jnp.where(same_cb, -cim_ref[0], 0.0)
    ab_ref[0, :, :sp] = jnp.broadcast_to(ab_re, (SUBLANES, sp))
    ab_ref[0, :, sp:] = jnp.broadcast_to(ab_im, (SUBLANES, sp))


def _ssm_prep(a_re, a_im, log_dt, b_re, b_im, c_re, c_im):
    depth, g, p = a_re.shape
    c = b_re.shape[-1]
    sw, sp = g * c, g * p
    assert c & (c - 1) == 0 and p & (p - 1) == 0 and sw % LANES == 0
    flat = lambda x: x.reshape(depth, 1, sp)
    ldt = jnp.repeat(log_dt, p, axis=1).reshape(depth, 1, sp)
    tile_b = lambda b: jnp.tile(b.transpose(0, 1, 3, 2).reshape(depth, sw, p), (1, 1, g))
    tile_c = lambda x: jnp.tile(x.transpose(0, 3, 1, 2).reshape(depth, p, sw), (1, g, 1))
    vec = pl.BlockSpec((1, 1, sp), lambda l, j: (l, 0, 0))
    bspec = pl.BlockSpec((1, LANES, sp), lambda l, j: (l, j, 0))
    cspec = pl.BlockSpec((1, sp, LANES), lambda l, j: (l, 0, j))
    return pl.pallas_call(
        functools.partial(_ssm_prep_kernel, c_shift=c.bit_length() - 1, p_shift=p.bit_length() - 1),
        grid=(depth, sw // LANES),
        in_specs=[vec, vec, vec, bspec, bspec, cspec, cspec],
        out_specs=[pl.BlockSpec((1, LANES, 2 * sp), lambda l, j: (l, j, 0)),
                   pl.BlockSpec((1, 2 * sp, LANES), lambda l, j: (l, 0, j)),
                   pl.BlockSpec((1, SUBLANES, 2 * sp), lambda l, j: (l, 0, 0))],
        out_shape=[jax.ShapeDtypeStruct((depth, sw, 2 * sp), F32),
                   jax.ShapeDtypeStruct((depth, 2 * sp, sw), F32),
                   jax.ShapeDtypeStruct((depth, SUBLANES, 2 * sp), F32)],
        compiler_params=_params("arbitrary", "arbitrary"),
        name="ssm_prep",
    )(flat(a_re), flat(a_im), ldt, tile_b(b_re), tile_b(b_im), tile_c(c_re), tile_c(c_im))


def _in_proj_kernel(x_ref, g_ref, wq_ref, wk_ref, wv_ref, wu_ref, q_ref, k_ref, v_ref, u_ref, *, kv_transposed):
    xn = _rms(x_ref[...], g_ref[...])
    if wq_ref.dtype == BF16:
        xn = xn.astype(BF16)
    q_ref[...] = _mm(xn, wq_ref[...])
    u_ref[...] = _mm(xn, wu_ref[...])
    if kv_transposed:
        k_ref[0] = lax.dot_general(wk_ref[...], xn, _NT, preferred_element_type=F32)
        v_ref[0] = lax.dot_general(wv_ref[...], xn, _NT, preferred_element_type=F32)
    else:
        k_ref[...] = _mm(xn, wk_ref[...])
        v_ref[...] = _mm(xn, wv_ref[...])


def _in_proj(x, g, lw, tm, prompt_batch=None):
    t, d = x.shape
    aw, sw = lw["w_q"].shape[1], lw["w_u"].shape[1]
    nt = t // tm
    row = pl.BlockSpec((tm, aw), lambda i: (i, 0))
    if prompt_batch is None:
        kv_spec, kv_shape = row, (t, aw)
        u_spec, u_shape = pl.BlockSpec((tm, sw), lambda i: (i, 0)), (t, sw)
    else:
        nlt = nt // prompt_batch
        kv_spec = pl.BlockSpec((1, aw, tm), lambda i: (i // nlt, 0, i % nlt))
        kv_shape = (prompt_batch, aw, t // prompt_batch)
        u_spec = pl.BlockSpec((tm, sw), lambda i: (i % nlt, i // nlt))
        u_shape = (t // prompt_batch, prompt_batch * sw)
    return pl.pallas_call(
        functools.partial(_in_proj_kernel, kv_transposed=prompt_batch is not None),
        grid=(nt,),
        in_specs=[pl.BlockSpec((tm, d), lambda i: (i, 0)), _const_spec((1, d)), _const_spec(lw["w_q"].shape),
                  _const_spec(lw["w_k"].shape), _const_spec(lw["w_v"].shape), _const_spec(lw["w_u"].shape)],
        out_specs=[row, kv_spec, kv_spec, u_spec],
        out_shape=[jax.ShapeDtypeStruct((t, aw), F32), jax.ShapeDtypeStruct(kv_shape, F32),
                   jax.ShapeDtypeStruct(kv_shape, F32), jax.ShapeDtypeStruct(u_shape, F32)],
        compiler_params=_params("parallel"),
        name="in_proj",
    )(x, g.reshape(1, d), lw["w_q"], lw["w_k"], lw["w_v"], lw["w_u"])


def _stick_terms(z2):
    sign = jnp.uint32(0x80000000)
    neg_abs = pltpu.bitcast(pltpu.bitcast(z2, jnp.uint32) | sign, F32)
    sp = jnp.maximum(z2, 0.0) + jnp.log2(1.0 + jnp.exp2(neg_abs))
    return sp, z2 - sp


def _suffix_sums(sp, u2_ref):
    hi = sp.astype(BF16)
    lo = (sp - hi.astype(F32)).astype(BF16)
    return jnp.dot(jnp.concatenate([hi, lo], axis=1), u2_ref[...], preferred_element_type=F32)


def _suffix_matrix():
    j = lax.broadcasted_iota(jnp.int32, (2 * LANES, 2 * LANES), 0) % LANES
    s = lax.broadcasted_iota(jnp.int32, (2 * LANES, 2 * LANES), 1)
    return jnp.where((s >= LANES) | (j > s), 1.0, 0.0).astype(BF16)


def _attn_prompt_kernel(bias_ref, q_ref, k_ref, v_ref, u2_ref, o_ref, kb_ref, vb_ref, acc_ref, car_ref, *, hd):
    tq = LANES
    seq = q_ref.shape[1]
    sup = kb_ref.shape[2] // LANES
    tk = sup * LANES
    sup_shift = sup.bit_length() - 1
    hp = pl.program_id(1)
    for s in range(seq // tk):
        cols = slice(s * tk, (s + 1) * tk)
        kb_ref[s] = k_ref[0, :, cols].astype(BF16)
        vb_ref[s] = v_ref[0, :, cols].T.astype(BF16)
    lane = lax.broadcasted_iota(jnp.int32, (tq, LANES), 1)
    first_head = lane < hd
    causal = lax.broadcasted_iota(jnp.int32, (tq, tq), 1) < lax.broadcasted_iota(jnp.int32, (tq, tq), 0)
    scale = hd ** -0.5 * LOG2E
    bias = [bias_ref[2 * hp] * LOG2E, bias_ref[2 * hp + 1] * LOG2E]
    heads = range(2)

    def group(qms, s, nb, diag):
        wd = nb * LANES
        blocks = lambda x: [x[:, k * LANES:(k + 1) * LANES] for k in range(nb)]
        z = [jnp.dot(qms[h], kb_ref[s, :, :wd], preferred_element_type=F32) + bias[h] for h in heads]
        terms = [_stick_terms(z[h]) for h in heads]
        sp = [blocks(terms[h][0]) for h in heads]
        ls = [blocks(terms[h][1]) for h in heads]
        if diag:
            for h in heads:
                sp[h][-1] = jnp.where(causal, sp[h][-1], 0.0)
        c = [_suffix_sums(jnp.concatenate(sp[h], axis=0), u2_ref) for h in heads]
        w = []
        for h in heads:
            carry = None if diag else car_ref[h]
            wh = [None] * nb
            for k in reversed(range(nb)):
                rows = slice(k * LANES, (k + 1) * LANES)
                t = ls[h][k] - c[h][rows, :LANES]
                if carry is not None:
                    t = t - carry
                e = jnp.exp2(t)
                if diag and k == nb - 1:
                    e = jnp.where(causal, e, 0.0)
                wh[k] = e.astype(BF16)
                tot = c[h][rows, LANES:]
                carry = tot if carry is None else carry + tot
            car_ref[h] = carry
            w.append(jnp.concatenate(wh, axis=1))
        pv = [jnp.dot(w[h], vb_ref[s, :wd, :], preferred_element_type=F32) for h in heads]
        for h in heads:
            acc_ref[h] = pv[h] if diag else acc_ref[h] + pv[h]

    def q_block(i, _):
        rows = pl.ds(pl.multiple_of(i * tq, tq), tq)
        q = q_ref[0, rows, :] * scale
        qms = [jnp.where(first_head, q, 0.0).astype(BF16), jnp.where(first_head, 0.0, q).astype(BF16)]
        s_own = lax.shift_right_logical(i, sup_shift)
        for nb in range(1, sup + 1):
            @pl.when((i & (sup - 1)) == nb - 1)
            def _(nb=nb):
                group(qms, s_own, nb, True)

        def older(ss, _):
            group(qms, s_own - 1 - ss, sup, False)
            return 0

        lax.fori_loop(0, s_own, older, 0)
        o_ref[0, rows, :] = jnp.where(first_head, acc_ref[0], acc_ref[1])
        return 0

    lax.fori_loop(0, seq // tq, q_block, 0)


def _attn_prompt(q, k_t, v_t, bias, u2, hd):
    b, seq, aw = q.shape
    assert 2 * hd == LANES and seq % LANES == 0 and aw % LANES == 0
    sup = max(s for s in (1, 2, ATTN_GROUP) if seq % (s * LANES) == 0 and s <= ATTN_GROUP)
    tk = sup * LANES
    blk = pl.BlockSpec((1, seq, LANES), lambda bi, hp: (bi, 0, hp))
    blk_t = pl.BlockSpec((1, LANES, seq), lambda bi, hp: (bi, hp, 0))
    return pl.pallas_call(
        functools.partial(_attn_prompt_kernel, hd=hd),
        grid=(b, aw // LANES),
        in_specs=[pl.BlockSpec(memory_space=pltpu.SMEM), blk, blk_t, blk_t,
                  _const_spec((2 * LANES, 2 * LANES))],
        out_specs=blk,
        out_shape=jax.ShapeDtypeStruct((b, seq, aw), F32),
        scratch_shapes=[pltpu.VMEM((seq // tk, LANES, tk), BF16), pltpu.VMEM((seq // tk, tk, LANES), BF16),
                        pltpu.VMEM((2, LANES, LANES), F32), pltpu.VMEM((2, LANES, LANES), F32)],
        compiler_params=_params("parallel", "parallel"),
        name="attn_prompt",
    )(bias, q, k_t, v_t, u2)


def _attn_decode_kernel(pt_ref, q_ref, bias_ref, *refs, hd, npp):
    del pt_ref
    k_refs, v_refs = refs[:npp], refs[npp:2 * npp]
    u2_ref, o_ref, qbd_ref, acc_ref, car_ref = refs[2 * npp:]
    j = pl.program_id(1)
    rows, aw = qbd_ref.shape
    own = (lax.broadcasted_iota(jnp.int32, (rows, aw), 1) // hd) == lax.broadcasted_iota(jnp.int32, (rows, aw), 0)

    @pl.when(j == 0)
    def _():
        q = jnp.broadcast_to(q_ref[0] * (hd ** -0.5 * LOG2E), (rows, aw))
        qbd_ref[...] = jnp.where(own, q, 0.0).astype(BF16)
        acc_ref[...] = jnp.zeros_like(acc_ref)
        car_ref[...] = jnp.zeros_like(car_ref)

    qbd = qbd_ref[...]
    bias2 = bias_ref[...] * LOG2E
    terms = [_stick_terms(jnp.dot(qbd, k_refs[p][0, 0].astype(BF16), preferred_element_type=F32) + bias2)
             for p in range(npp)]
    c = _suffix_sums(jnp.concatenate([t[0] for t in terms], axis=0), u2_ref)
    carry = car_ref[...]
    pv = None
    for p in range(npp):
        r = slice(p * rows, (p + 1) * rows)
        w = jnp.exp2(terms[p][1] - c[r, :LANES] - carry).astype(BF16)
        carry = carry + c[r, LANES:]
        part = lax.dot_general(w, v_refs[p][0, 0].astype(BF16), _NT, preferred_element_type=F32)
        pv = part if pv is None else pv + part
    car_ref[...] = carry
    acc_ref[...] += pv

    @pl.when(j == pl.num_programs(1) - 1)
    def _():
        o_ref[0] = jnp.sum(jnp.where(own, acc_ref[...], 0.0), axis=0, keepdims=True)


def _attn_decode(q, bias, cache_k, cache_v, page_table, layer, u2, hd):
    s, aw = q.shape
    n_pages = page_table.shape[1]
    page = cache_k.shape[3]
    nh = aw // hd
    rows = 2 * SUBLANES
    assert page == LANES and nh <= rows
    bias_rows = jnp.broadcast_to(jnp.pad(bias, (0, rows - nh))[:, None], (rows, LANES))
    npp = max(n for n in range(1, DECODE_PAGES + 1) if n_pages % n == 0)

    def kv_spec(p):
        return pl.BlockSpec((1, 1, aw, page), lambda b, j, pt: (layer, pt[b, n_pages - 1 - (j * npp + p)], 0, 0))

    q_spec = pl.BlockSpec((1, 1, aw), lambda b, j, pt: (b, 0, 0))
    pages = [kv_spec(p) for p in range(npp)]
    out = pl.pallas_call(
        functools.partial(_attn_decode_kernel, hd=hd, npp=npp),
        grid_spec=pltpu.PrefetchScalarGridSpec(
            num_scalar_prefetch=1,
            grid=(s, n_pages // npp),
            in_specs=[q_spec, pl.BlockSpec((rows, LANES), lambda b, j, pt: (0, 0))] + pages + pages +
                     [pl.BlockSpec((2 * LANES, 2 * LANES), lambda b, j, pt: (0, 0))],
            out_specs=q_spec,
            scratch_shapes=[pltpu.VMEM((rows, aw), BF16), pltpu.VMEM((rows, aw), F32),
                            pltpu.VMEM((rows, LANES), F32)],
        ),
        out_shape=jax.ShapeDtypeStruct((s, 1, aw), F32),
        compiler_params=_params("parallel", "arbitrary"),
        name="attn_decode",
    )(page_table, q.reshape(s, 1, aw), bias_rows, *([cache_k] * npp), *([cache_v] * npp), u2)
    return out.reshape(s, aw)


def _ssm_kernel(u_ref, bbd_ref, cbd_ref, ab_ref, d_ref, wglu_ref, h0_ref, y_ref, ht_ref, bu_ref, h_ref,
                *, rows, steps, cw):
    c = pl.program_id(0)
    sp = h_ref.shape[1] // 2

    @pl.when(c == 0)
    def _():
        h_ref[...] = h0_ref[...]

    u = u_ref[...]
    bu_ref[...] = _mm(u, bbd_ref[...])
    for cc in range(sp // cw):
        re = slice(cc * cw, (cc + 1) * cw)
        im = slice(sp + cc * cw, sp + (cc + 1) * cw)
        a_r = jnp.broadcast_to(ab_ref[0:1, re], (rows, cw))
        a_i = jnp.broadcast_to(ab_ref[0:1, im], (rows, cw))

        def step(t, carry):
            h_r, h_i = carry
            r = pl.ds(pl.multiple_of(t * rows, rows), rows)
            n_r = a_r * h_r - a_i * h_i + bu_ref[r, re]
            n_i = a_r * h_i + a_i * h_r + bu_ref[r, im]
            bu_ref[r, re] = n_r
            bu_ref[r, im] = n_i
            return n_r, n_i

        h_r, h_i = lax.fori_loop(0, steps, step, (h_ref[:, re], h_ref[:, im]), unroll=min(steps, 8))
        h_ref[:, re] = h_r
        h_ref[:, im] = h_i

    y = _mm(bu_ref[...], cbd_ref[...]) + d_ref[...] * u
    y = jax.nn.gelu(y, approximate=True)
    y_ref[...] = y * jax.nn.sigmoid(_mm(y, wglu_ref[...]))

    @pl.when(c == pl.num_programs(0) - 1)
    def _():
        ht_ref[...] = h_ref[...]


def _ssm(u, bbd, cbd, ab, d_skip, w_glu, h0, rows, steps):
    n, sw = u.shape
    sp2 = bbd.shape[1]
    chunk = rows * steps
    cw = 4 * LANES * SUBLANES // rows if rows <= 4 * SUBLANES else LANES
    cw = max(LANES, min(cw, sp2 // 2))
    blk = pl.BlockSpec((chunk, sw), lambda c: (c, 0))
    return pl.pallas_call(
        functools.partial(_ssm_kernel, rows=rows, steps=steps, cw=cw),
        grid=(n // chunk,),
        in_specs=[blk, _const_spec((sw, sp2)), _const_spec((sp2, sw)), _const_spec((SUBLANES, sp2)),
                  _const_spec((1, sw)), _const_spec((sw, sw)), _const_spec((rows, sp2))],
        out_specs=[blk, pl.BlockSpec((rows, sp2), lambda c: (0, 0))],
        out_shape=[jax.ShapeDtypeStruct((n, sw), F32), jax.ShapeDtypeStruct((rows, sp2), F32)],
        scratch_shapes=[pltpu.VMEM((chunk, sp2), F32), pltpu.VMEM((rows, sp2), F32)],
        compiler_params=_params("arbitrary"),
        name="ssm",
    )(u, bbd, cbd, ab, d_skip.reshape(1, sw), w_glu, h0)


def _tail_kernel(h_ref, att_ref, ssm_ref, p_ref, ga_ref, gs_ref, woa_ref, wos_ref, gf_ref, wup_ref, wdn_ref,
                 gp_ref, wg_ref, wp_ref, gfin_ref, o_ref, acc_ref, xn_ref, *, n_sub, final):
    c = pl.program_id(1)

    @pl.when(c == 0)
    def _():
        na = _rms(att_ref[...], ga_ref[...])
        ns = _rms(ssm_ref[...], gs_ref[...])
        h1 = h_ref[...] + _mm(na, woa_ref[...]) + _mm(ns, wos_ref[...])
        acc_ref[...] = h1
        xn_ref[...] = _rms(h1, gf_ref[...]).astype(xn_ref.dtype)

    xn = xn_ref[...]
    sub = wup_ref.shape[1] // n_sub
    ffn = None
    for s in range(n_sub):
        f = _mm(xn, wup_ref[:, s * sub:(s + 1) * sub])
        part = _mm(jnp.square(jnp.maximum(f, 0.0)), wdn_ref[s * sub:(s + 1) * sub, :])
        ffn = part if ffn is None else ffn + part
    acc_ref[...] += ffn

    @pl.when(c == pl.num_programs(1) - 1)
    def _():
        h2 = acc_ref[...]
        gate = jax.nn.sigmoid(_mm(_rms(h2, gp_ref[...]), wg_ref[...]))
        h3 = h2 + _mm(p_ref[...], wp_ref[...]) * gate
        o_ref[...] = _rms(h3, gfin_ref[...]) if final else h3


def _tail(h, att, ssm, p, lw, g_final, tm, ff_chunks, n_sub, final, time_major_batch=None):
    t, d = h.shape
    aw, sw, pd = att.shape[1], lw["w_out_s"].shape[0], p.shape[1]
    d_ff = lw["w_up"].shape[1]
    ffc = d_ff // ff_chunks
    nt = t // tm
    wdt = lw["w_up"].dtype
    if time_major_batch is None:
        ssm_spec = pl.BlockSpec((tm, sw), lambda i, c: (i, 0))
    else:
        nlt = nt // time_major_batch
        ssm_spec = pl.BlockSpec((tm, sw), lambda i, c: (i % nlt, i // nlt))
    tok = lambda w: pl.BlockSpec((tm, w), lambda i, c: (i, 0))
    if ff_chunks == 1:
        up_spec, dn_spec = _const_spec((d, d_ff)), _const_spec((d_ff, d))
    else:
        up_spec = pl.BlockSpec((d, ffc), lambda i, c: (0, c))
        dn_spec = pl.BlockSpec((ffc, d), lambda i, c: (c, 0))
    vec = lambda g: g.reshape(1, -1)
    return pl.pallas_call(
        functools.partial(_tail_kernel, n_sub=n_sub, final=final),
        grid=(nt, ff_chunks),
        in_specs=[tok(d), tok(aw), ssm_spec, tok(pd),
                  _const_spec((1, aw)), _const_spec((1, sw)), _const_spec((aw, d)), _const_spec((sw, d)),
                  _const_spec((1, d)), up_spec, dn_spec,
                  _const_spec((1, d)), _const_spec((d, d)), _const_spec((pd, d)), _const_spec((1, d))],
        out_specs=tok(d),
        out_shape=jax.ShapeDtypeStruct((t, d), F32),
        scratch_shapes=[pltpu.VMEM((tm, d), F32), pltpu.VMEM((tm, d), wdt)],
        compiler_params=_params("parallel", "arbitrary"),
        name="tail",
    )(h, att, ssm, p, vec(lw["g_att_out"]), vec(lw["g_ssm_out"]), lw["w_out_a"], lw["w_out_s"],
      vec(lw["g_ffn"]), lw["w_up"], lw["w_down"], vec(lw["g_ple"]), lw["w_ple_gate"], lw["w_ple"], vec(g_final))


PROMPT_TILE = 512
SSM_STEPS = 64
FF_SUB = 1024


def kernel(x_prompt, x_sample, p_prompt, p_sample, cache_k, cache_v, state_ssm_re, state_ssm_im, page_table, w_in, w_out, g_mix, g_att_out, g_ssm_out, att_bias, ssm_a_re, ssm_a_im, ssm_log_dt, ssm_b_re, ssm_b_im, ssm_c_re, ssm_c_im, ssm_d, w_glu, g_ffn, w_up, w_down, w_ple, w_ple_gate, g_ple, g_final):
    b, seq, d = x_prompt.shape
    sb, st = x_sample.shape[0], x_sample.shape[1]
    assert st == 1 and b % SUBLANES == 0
    depth, n_groups, n_state = ssm_a_re.shape
    n_heads = att_bias.shape[1]
    hd = cache_k.shape[-1]
    aw = n_heads * hd
    sw = n_groups * ssm_b_re.shape[-1]
    sp = n_groups * n_state
    d_ff = w_up.shape[2]
    tm = min(PROMPT_TILE, seq)
    steps = min(SSM_STEPS, seq)
    n_sub = max(1, d_ff // FF_SUB)

    bbd, cbd, ab = _ssm_prep(ssm_a_re, ssm_a_im, ssm_log_dt, ssm_b_re, ssm_b_im, ssm_c_re, ssm_c_im)
    u2 = _suffix_matrix()
    pages_t = lambda c: c.transpose(0, 1, 3, 4, 2).reshape(depth, c.shape[1], aw, c.shape[2])
    cache_k, cache_v = pages_t(cache_k), pages_t(cache_v)

    def layer_weights(l, dt, kv_transposed):
        cast = lambda w: w[l].astype(dt)
        kv = (lambda w: cast(w).T) if kv_transposed else cast
        return {"w_q": cast(w_in[:, :, :aw]), "w_k": kv(w_in[:, :, aw:2 * aw]), "w_v": kv(w_in[:, :, 2 * aw:3 * aw]),
                "w_u": cast(w_in[:, :, 3 * aw:]), "w_out_a": cast(w_out[:, :aw]), "w_out_s": cast(w_out[:, aw:]),
                "w_up": cast(w_up), "w_down": cast(w_down), "w_ple": cast(w_ple),
                "w_ple_gate": cast(w_ple_gate), "w_glu": cast(w_glu), "bbd": cast(bbd), "cbd": cast(cbd),
                "g_att_out": g_att_out[l], "g_ssm_out": g_ssm_out[l], "g_ffn": g_ffn[l], "g_ple": g_ple[l]}

    hp = x_prompt.reshape(b * seq, d)
    hs = x_sample.reshape(sb, d)
    zero_state = jnp.zeros((b, 2 * sp), F32)
    outs = {name: [] for name in ("kp", "vp", "rp", "ip", "ks", "vs", "rs", "is")}
    for l in range(depth):
        final = l == depth - 1
        lw = layer_weights(l, BF16, True)
        q, k, v, u = _in_proj(hp, g_mix[l], lw, tm, prompt_batch=b)
        att = _attn_prompt(q.reshape(b, seq, aw), k, v, att_bias[l], u2, hd)
        y, h_last = _ssm(u.reshape(seq * b, sw), lw["bbd"], lw["cbd"], ab[l], ssm_d[l].reshape(-1),
                         lw["w_glu"], zero_state, rows=b, steps=steps)
        hp = _tail(hp, att.reshape(b * seq, aw), y.reshape(seq, b * sw), p_prompt[l].reshape(b * seq, -1),
                   lw, g_final, tm, 1, n_sub, final, time_major_batch=b)
        outs["kp"].append(k.reshape(b, n_heads, hd, seq))
        outs["vp"].append(v.reshape(b, n_heads, hd, seq))
        outs["rp"].append(h_last[:, :sp].reshape(b, n_groups, n_state))
        outs["ip"].append(h_last[:, sp:].reshape(b, n_groups, n_state))

        lw = layer_weights(l, F32, False)
        q, k, v, u = _in_proj(hs, g_mix[l], lw, sb)
        att = _attn_decode(q, att_bias[l], cache_k, cache_v, page_table, l, u2, hd)
        h0 = jnp.concatenate([state_ssm_re[l].reshape(sb, sp), state_ssm_im[l].reshape(sb, sp)], axis=1)
        y, h_last = _ssm(u, lw["bbd"], lw["cbd"], ab[l], ssm_d[l].reshape(-1), lw["w_glu"], h0,
                         rows=sb, steps=1)
        hs = _tail(hs, att, y, p_sample[l].reshape(sb, -1), lw, g_final, sb, n_sub, 1, final)
        outs["ks"].append(k.reshape(sb, 1, n_heads, hd))
        outs["vs"].append(v.reshape(sb, 1, n_heads, hd))
        outs["rs"].append(h_last[:, :sp].reshape(sb, n_groups, n_state))
        outs["is"].append(h_last[:, sp:].reshape(sb, n_groups, n_state))

    stack = lambda name: jnp.stack(outs[name])
    seq_major = lambda x: x.transpose(0, 1, 4, 2, 3)
    return (hp.reshape(b, seq, d), hs.reshape(sb, st, d),
            seq_major(stack("kp")), seq_major(stack("vp")), stack("rp"), stack("ip"),
            stack("ks"), stack("vs"), stack("rs"), stack("is"))
```

```python
import functools

import jax
import jax.numpy as jnp
from jax import lax
from jax.experimental import pallas as pl
from jax.experimental.pallas import tpu as pltpu

RMS_EPS = 1e-6
LANES = 128
SUBLANES = 8
VMEM_LIMIT = 56 * 1024 * 1024
BF16 = jnp.bfloat16
F32 = jnp.float32
_NT = (((1,), (1,)), ((), ()))
LOG2E = 1.4426950408889634
ATTN_GROUP = 4
ATTN_Q_ROWS = 256
DECODE_PAGES = 16


def _mm(a, w):
    if w.dtype == BF16:
        return jnp.dot(a.astype(BF16), w, preferred_element_type=F32)
    return jnp.dot(a, w, preferred_element_type=F32, precision=lax.Precision.HIGHEST)


def _rms(x, g):
    return x * lax.rsqrt(jnp.mean(x * x, axis=-1, keepdims=True) + RMS_EPS) * g


def _const_spec(shape):
    nd = len(shape)
    return pl.BlockSpec(shape, lambda *_: (0,) * nd, pipeline_mode=pl.Buffered(1))


def _params(*sem):
    return pltpu.CompilerParams(dimension_semantics=sem, vmem_limit_bytes=VMEM_LIMIT)


def _ssm_prep_kernel(are_ref, aim_ref, ldt_ref, bre_ref, bim_ref, cre_ref, cim_ref,
                     bbd_ref, cbd_ref, ab_ref, *, c_shift, p_shift):
    a_re = are_ref[0]
    a_im = aim_ref[0]
    dt = jnp.exp(ldt_ref[0])
    mag = jnp.exp(a_re * dt)
    ab_re = mag * jnp.cos(a_im * dt)
    ab_im = mag * jnp.sin(a_im * dt)
    den = a_re * a_re + a_im * a_im
    f_re = ((ab_re - 1.0) * a_re + ab_im * a_im) / den
    f_im = (ab_im * a_re - (ab_re - 1.0) * a_im) / den
    cb, sp = bre_ref.shape[1], bre_ref.shape[2]
    ch0 = pl.program_id(1) * cb
    same_bc = ((lax.broadcasted_iota(jnp.int32, (cb, sp), 0) + ch0) >> c_shift) == \
              (lax.broadcasted_iota(jnp.int32, (cb, sp), 1) >> p_shift)
    br = bre_ref[0]
    bi = bim_ref[0]
    bbd_ref[0, :, :sp] = jnp.where(same_bc, f_re * br - f_im * bi, 0.0)
    bbd_ref[0, :, sp:] = jnp.where(same_bc, f_re * bi + f_im * br, 0.0)
    same_cb = (lax.broadcasted_iota(jnp.int32, (sp, cb), 0) >> p_shift) == \
              ((lax.broadcasted_iota(jnp.int32, (sp, cb), 1) + ch0) >> c_shift)
    cbd_ref[0, :sp, :] = jnp.where(same_cb, cre_ref[0], 0.0)
    cbd_ref[0, sp:, :] = jnp.where(same_cb, -cim_ref[0], 0.0)
    ab_ref[0, :, :sp] = jnp.broadcast_to(ab_re, (SUBLANES, sp))
    ab_ref[0, :, sp:] = jnp.broadcast_to(ab_im, (SUBLANES, sp))


def _ssm_prep(a_re, a_im, log_dt, b_re, b_im, c_re, c_im):
    depth, g, p = a_re.shape
    c = b_re.shape[-1]
    sw, sp = g * c, g * p
    assert c & (c - 1) == 0 and p & (p - 1) == 0 and sw % LANES == 0
    flat = lambda x: x.reshape(depth, 1, sp)
    ldt = jnp.repeat(log_dt, p, axis=1).reshape(depth, 1, sp)
    tile_b = lambda b: jnp.tile(b.transpose(0, 1, 3, 2).reshape(depth, sw, p), (1, 1, g))
    tile_c = lambda x: jnp.tile(x.transpose(0, 3, 1, 2).reshape(depth, p, sw), (1, g, 1))
    vec = pl.BlockSpec((1, 1, sp), lambda l, j: (l, 0, 0))
    bspec = pl.BlockSpec((1, LANES, sp), lambda l, j: (l, j, 0))
    cspec = pl.BlockSpec((1, sp, LANES), lambda l, j: (l, 0, j))
    return pl.pallas_call(
        functools.partial(_ssm_prep_kernel, c_shift=c.bit_length() - 1, p_shift=p.bit_length() - 1),
        grid=(depth, sw // LANES),
        in_specs=[vec, vec, vec, bspec, bspec, cspec, cspec],
        out_specs=[pl.BlockSpec((1, LANES, 2 * sp), lambda l, j: (l, j, 0)),
                   pl.BlockSpec((1, 2 * sp, LANES), lambda l, j: (l, 0, j)),
                   pl.BlockSpec((1, SUBLANES, 2 * sp), lambda l, j: (l, 0, 0))],
        out_shape=[jax.ShapeDtypeStruct((depth, sw, 2 * sp), F32),
                   jax.ShapeDtypeStruct((depth, 2 * sp, sw), F32),
                   jax.ShapeDtypeStruct((depth, SUBLANES, 2 * sp), F32)],
        compiler_params=_params("arbitrary", "arbitrary"),
        name="ssm_prep",
    )(flat(a_re), flat(a_im), ldt, tile_b(b_re), tile_b(b_im), tile_c(c_re), tile_c(c_im))


def _in_proj_kernel(x_ref, g_ref, wq_ref, wk_ref, wv_ref, wu_ref, q_ref, k_ref, v_ref, u_ref, *, kv_transposed):
    xn = _rms(x_ref[...], g_ref[...])
    if wq_ref.dtype == BF16:
        xn = xn.astype(BF16)
    q_ref[...] = _mm(xn, wq_ref[...])
    u_ref[...] = _mm(xn, wu_ref[...])
    if kv_transposed:
        k_ref[0] = lax.dot_general(wk_ref[...], xn, _NT, preferred_element_type=F32)
        v_ref[0] = lax.dot_general(wv_ref[...], xn, _NT, preferred_element_type=F32)
    else:
        k_ref[...] = _mm(xn, wk_ref[...])
        v_ref[...] = _mm(xn, wv_ref[...])


def _in_proj(x, g, lw, tm, prompt_batch=None):
    t, d = x.shape
    aw, sw = lw["w_q"].shape[1], lw["w_u"].shape[1]
    nt = t // tm
    row = pl.BlockSpec((tm, aw), lambda i: (i, 0))
    if prompt_batch is None:
        kv_spec, kv_shape = row, (t, aw)
        u_spec, u_shape = pl.BlockSpec((tm, sw), lambda i: (i, 0)), (t, sw)
    else:
        nlt = nt // prompt_batch
        kv_spec = pl.BlockSpec((1, aw, tm), lambda i: (i // nlt, 0, i % nlt))
        kv_shape = (prompt_batch, aw, t // prompt_batch)
        u_spec = pl.BlockSpec((tm, sw), lambda i: (i % nlt, i // nlt))
        u_shape = (t // prompt_batch, prompt_batch * sw)
    return pl.pallas_call(
        functools.partial(_in_proj_kernel, kv_transposed=prompt_batch is not None),
        grid=(nt,),
        in_specs=[pl.BlockSpec((tm, d), lambda i: (i, 0)), _const_spec((1, d)), _const_spec(lw["w_q"].shape),
                  _const_spec(lw["w_k"].shape), _const_spec(lw["w_v"].shape), _const_spec(lw["w_u"].shape)],
        out_specs=[row, kv_spec, kv_spec, u_spec],
        out_shape=[jax.ShapeDtypeStruct((t, aw), F32), jax.ShapeDtypeStruct(kv_shape, F32),
                   jax.ShapeDtypeStruct(kv_shape, F32), jax.ShapeDtypeStruct(u_shape, F32)],
        compiler_params=_params("parallel"),
        name="in_proj",
    )(x, g.reshape(1, d), lw["w_q"], lw["w_k"], lw["w_v"], lw["w_u"])


def _stick_terms(z2):
    sp = jnp.maximum(z2, 0.0) + jnp.log2(1.0 + jnp.exp2(-jnp.abs(z2)))
    return sp, z2 - sp


def _suffix_sums(sp, u2_ref):
    hi = sp.astype(BF16)
    lo = (sp - hi.astype(F32)).astype(BF16)
    return jnp.dot(jnp.concatenate([hi, lo], axis=1), u2_ref[...], preferred_element_type=F32)


def _suffix_matrix():
    j = lax.broadcasted_iota(jnp.int32, (2 * LANES, 2 * LANES), 0) % LANES
    s = lax.broadcasted_iota(jnp.int32, (2 * LANES, 2 * LANES), 1)
    return jnp.where((s >= LANES) | (j > s), 1.0, 0.0).astype(BF16)


def _attn_prompt_kernel(bias_ref, q_ref, k_ref, v_ref, u2_ref, o_ref, kb_ref, vb_ref, acc_ref, car_ref, *, hd):
    tq = acc_ref.shape[1]
    qb = tq // LANES
    seq = q_ref.shape[1]
    sup = kb_ref.shape[2] // LANES
    tk = sup * LANES
    sup_shift = sup.bit_length() - 1
    hp = pl.program_id(1)
    for s in range(seq // tk):
        cols = slice(s * tk, (s + 1) * tk)
        kb_ref[s] = k_ref[0, :, cols].astype(BF16)
        vb_ref[s] = v_ref[0, :, cols].T.astype(BF16)
    lane = lax.broadcasted_iota(jnp.int32, (tq, LANES), 1)
    first_head = lane < hd
    causal = lax.broadcasted_iota(jnp.int32, (tq, tq), 1) < lax.broadcasted_iota(jnp.int32, (tq, tq), 0)
    scale = hd ** -0.5 * LOG2E
    bias = [bias_ref[2 * hp] * LOG2E, bias_ref[2 * hp + 1] * LOG2E]
    heads = range(2)

    def group(qms, s, nb, diag):
        wd = nb * LANES
        blocks = lambda x: [x[:, k * LANES:(k + 1) * LANES] for k in range(nb)]
        own = lambda k: causal[:, (k - (nb - qb)) * LANES:(k - (nb - qb) + 1) * LANES]
        z = [jnp.dot(qms[h], kb_ref[s, :, :wd], preferred_element_type=F32) + bias[h] for h in heads]
        terms = [_stick_terms(z[h]) for h in heads]
        sp = [blocks(terms[h][0]) for h in heads]
        ls = [blocks(terms[h][1]) for h in heads]
        if diag:
            for h in heads:
                for k in range(nb - qb, nb):
                    sp[h][k] = jnp.where(own(k), sp[h][k], 0.0)
        c = [_suffix_sums(jnp.concatenate(sp[h], axis=0), u2_ref) for h in heads]
        w = []
        for h in heads:
            carry = None if diag else car_ref[h]
            wh = [None] * nb
            for k in reversed(range(nb)):
                rows = slice(k * tq, (k + 1) * tq)
                t = ls[h][k] - c[h][rows, :LANES]
                if carry is not None:
                    t = t - carry
                e = jnp.exp2(t)
                if diag and k >= nb - qb:
                    e = jnp.where(own(k), e, 0.0)
                wh[k] = e.astype(BF16)
                tot = c[h][rows, LANES:]
                carry = tot if carry is None else carry + tot
            car_ref[h] = carry
            w.append(jnp.concatenate(wh, axis=1))
        pv = [jnp.dot(w[h], vb_ref[s, :wd, :], preferred_element_type=F32) for h in heads]
        for h in heads:
            acc_ref[h] = pv[h] if diag else acc_ref[h] + pv[h]

    def q_block(i, _):
        rows = pl.ds(pl.multiple_of(i * tq, tq), tq)
        q = q_ref[0, rows, :] * scale
        qms = [jnp.where(first_head, q, 0.0).astype(BF16), jnp.where(first_head, 0.0, q).astype(BF16)]
        last = i * qb + (qb - 1)
        s_own = lax.shift_right_logical(last, sup_shift)
        for nb in range(qb, sup + 1, qb):
            @pl.when((last & (sup - 1)) == nb - 1)
            def _(nb=nb):
                group(qms, s_own, nb, True)

        def older(ss, _):
            group(qms, s_own - 1 - ss, sup, False)
            return 0

        lax.fori_loop(0, s_own, older, 0)
        o_ref[0, rows, :] = jnp.where(first_head, acc_ref[0], acc_ref[1])
        return 0

    lax.fori_loop(0, seq // tq, q_block, 0)


def _attn_prompt(q, k_t, v_t, bias, u2, hd):
    b, seq, aw = q.shape
    assert 2 * hd == LANES and seq % LANES == 0 and aw % LANES == 0
    sup = max(s for s in (1, 2, ATTN_GROUP) if seq % (s * LANES) == 0 and s <= ATTN_GROUP)
    tk = sup * LANES
    tq = ATTN_Q_ROWS if tk % ATTN_Q_ROWS == 0 else LANES
    blk = pl.BlockSpec((1, seq, LANES), lambda bi, hp: (bi, 0, hp))
    blk_t = pl.BlockSpec((1, LANES, seq), lambda bi, hp: (bi, hp, 0))
    return pl.pallas_call(
        functools.partial(_attn_prompt_kernel, hd=hd),
        grid=(b, aw // LANES),
        in_specs=[pl.BlockSpec(memory_space=pltpu.SMEM), blk, blk_t, blk_t,
                  _const_spec((2 * LANES, 2 * LANES))],
        out_specs=blk,
        out_shape=jax.ShapeDtypeStruct((b, seq, aw), F32),
        scratch_shapes=[pltpu.VMEM((seq // tk, LANES, tk), BF16), pltpu.VMEM((seq // tk, tk, LANES), BF16),
                        pltpu.VMEM((2, tq, LANES), F32), pltpu.VMEM((2, tq, LANES), F32)],
        compiler_params=_params("parallel", "parallel"),
        name="attn_prompt",
    )(bias, q, k_t, v_t, u2)


def _attn_decode_kernel(pt_ref, q_ref, bias_ref, *refs, hd, npp):
    del pt_ref
    k_refs, v_refs = refs[:npp], refs[npp:2 * npp]
    u2_ref, o_ref, qbd_ref, acc_ref, car_ref = refs[2 * npp:]
    j = pl.program_id(1)
    rows, aw = qbd_ref.shape
    own = (lax.broadcasted_iota(jnp.int32, (rows, aw), 1) // hd) == lax.broadcasted_iota(jnp.int32, (rows, aw), 0)

    @pl.when(j == 0)
    def _():
        q = jnp.broadcast_to(q_ref[0] * (hd ** -0.5 * LOG2E), (rows, aw))
        qbd_ref[...] = jnp.where(own, q, 0.0).astype(BF16)
        acc_ref[...] = jnp.zeros_like(acc_ref)
        car_ref[...] = jnp.zeros_like(car_ref)

    qbd = qbd_ref[...]
    bias2 = bias_ref[...] * LOG2E
    terms = [_stick_terms(jnp.dot(qbd, k_refs[p][0, 0].astype(BF16), preferred_element_type=F32) + bias2)
             for p in range(npp)]
    c = _suffix_sums(jnp.concatenate([t[0] for t in terms], axis=0), u2_ref)
    carry = car_ref[...]
    pv = None
    for p in range(npp):
        r = slice(p * rows, (p + 1) * rows)
        w = jnp.exp2(terms[p][1] - c[r, :LANES] - carry).astype(BF16)
        carry = carry + c[r, LANES:]
        part = lax.dot_general(w, v_refs[p][0, 0].astype(BF16), _NT, preferred_element_type=F32)
        pv = part if pv is None else pv + part
    car_ref[...] = carry
    acc_ref[...] += pv

    @pl.when(j == pl.num_programs(1) - 1)
    def _():
        o_ref[0] = jnp.sum(jnp.where(own, acc_ref[...], 0.0), axis=0, keepdims=True)


def _attn_decode(q, bias, cache_k, cache_v, page_table, layer, u2, hd):
    s, aw = q.shape
    n_pages = page_table.shape[1]
    page = cache_k.shape[3]
    nh = aw // hd
    rows = 2 * SUBLANES
    assert page == LANES and nh <= rows
    bias_rows = jnp.broadcast_to(jnp.pad(bias, (0, rows - nh))[:, None], (rows, LANES))
    npp = max(n for n in range(1, DECODE_PAGES + 1) if n_pages % n == 0)

    def kv_spec(p):
        return pl.BlockSpec((1, 1, aw, page), lambda b, j, pt: (layer, pt[b, n_pages - 1 - (j * npp + p)], 0, 0))

    q_spec = pl.BlockSpec((1, 1, aw), lambda b, j, pt: (b, 0, 0))
    pages = [kv_spec(p) for p in range(npp)]
    out = pl.pallas_call(
        functools.partial(_attn_decode_kernel, hd=hd, npp=npp),
        grid_spec=pltpu.PrefetchScalarGridSpec(
            num_scalar_prefetch=1,
            grid=(s, n_pages // npp),
            in_specs=[q_spec, pl.BlockSpec((rows, LANES), lambda b, j, pt: (0, 0))] + pages + pages +
                     [pl.BlockSpec((2 * LANES, 2 * LANES), lambda b, j, pt: (0, 0))],
            out_specs=q_spec,
            scratch_shapes=[pltpu.VMEM((rows, aw), BF16), pltpu.VMEM((rows, aw), F32),
                            pltpu.VMEM((rows, LANES), F32)],
        ),
        out_shape=jax.ShapeDtypeStruct((s, 1, aw), F32),
        compiler_params=_params("parallel", "arbitrary"),
        name="attn_decode",
    )(page_table, q.reshape(s, 1, aw), bias_rows, *([cache_k] * npp), *([cache_v] * npp), u2)
    return out.reshape(s, aw)


def _ssm_kernel(u_ref, bbd_ref, cbd_ref, ab_ref, d_ref, wglu_ref, h0_ref, y_ref, ht_ref, bu_ref, h_ref,
                *, rows, steps, cw):
    c = pl.program_id(0)
    sp = h_ref.shape[1] // 2

    @pl.when(c == 0)
    def _():
        h_ref[...] = h0_ref[...]

    sw = u_ref.shape[1]
    halves = [(slice(k * (sw // 2), (k + 1) * (sw // 2)), k * (sp // 2), (k + 1) * (sp // 2)) for k in range(2)]
    u = u_ref[...]
    for ch, lo, hi in halves:
        for off in (0, sp):
            bu_ref[:, off + lo:off + hi] = _mm(u[:, ch], bbd_ref[ch, off + lo:off + hi])
    for cc in range(sp // cw):
        re = slice(cc * cw, (cc + 1) * cw)
        im = slice(sp + cc * cw, sp + (cc + 1) * cw)
        a_r = jnp.broadcast_to(ab_ref[0:1, re], (rows, cw))
        a_i = jnp.broadcast_to(ab_ref[0:1, im], (rows, cw))

        def step(t, carry):
            h_r, h_i = carry
            r = pl.ds(pl.multiple_of(t * rows, rows), rows)
            n_r = a_r * h_r - a_i * h_i + bu_ref[r, re]
            n_i = a_r * h_i + a_i * h_r + bu_ref[r, im]
            bu_ref[r, re] = n_r
            bu_ref[r, im] = n_i
            return n_r, n_i

        h_r, h_i = lax.fori_loop(0, steps, step, (h_ref[:, re], h_ref[:, im]), unroll=min(steps, 8))
        h_ref[:, re] = h_r
        h_ref[:, im] = h_i

    y = jnp.concatenate(
        [_mm(bu_ref[:, lo:hi], cbd_ref[lo:hi, ch]) + _mm(bu_ref[:, sp + lo:sp + hi], cbd_ref[sp + lo:sp + hi, ch])
         for ch, lo, hi in halves], axis=1) + d_ref[...] * u
    y = jax.nn.gelu(y, approximate=True)
    y_ref[...] = y * jax.nn.sigmoid(_mm(y, wglu_ref[...]))

    @pl.when(c == pl.num_programs(0) - 1)
    def _():
        ht_ref[...] = h_ref[...]


def _ssm(u, bbd, cbd, ab, d_skip, w_glu, h0, rows, steps):
    n, sw = u.shape
    sp2 = bbd.shape[1]
    chunk = rows * steps
    cw = 4 * LANES * SUBLANES // rows if rows <= 4 * SUBLANES else LANES
    cw = max(LANES, min(cw, sp2 // 2))
    blk = pl.BlockSpec((chunk, sw), lambda c: (c, 0))
    return pl.pallas_call(
        functools.partial(_ssm_kernel, rows=rows, steps=steps, cw=cw),
        grid=(n // chunk,),
        in_specs=[blk, _const_spec((sw, sp2)), _const_spec((sp2, sw)), _const_spec((SUBLANES, sp2)),
                  _const_spec((1, sw)), _const_spec((sw, sw)), _const_spec((rows, sp2))],
        out_specs=[blk, pl.BlockSpec((rows, sp2), lambda c: (0, 0))],
        out_shape=[jax.ShapeDtypeStruct((n, sw), F32), jax.ShapeDtypeStruct((rows, sp2), F32)],
        scratch_shapes=[pltpu.VMEM((chunk, sp2), F32), pltpu.VMEM((rows, sp2), F32)],
        compiler_params=_params("arbitrary"),
        name="ssm",
    )(u, bbd, cbd, ab, d_skip.reshape(1, sw), w_glu, h0)


def _tail_kernel(h_ref, att_ref, ssm_ref, p_ref, ga_ref, gs_ref, woa_ref, wos_ref, gf_ref, wup_ref, wdn_ref,
                 gp_ref, wg_ref, wp_ref, gfin_ref, o_ref, acc_ref, xn_ref, *, n_sub, final):
    c = pl.program_id(1)

    @pl.when(c == 0)
    def _():
        na = _rms(att_ref[...], ga_ref[...])
        ns = _rms(ssm_ref[...], gs_ref[...])
        h1 = h_ref[...] + _mm(na, woa_ref[...]) + _mm(ns, wos_ref[...])
        acc_ref[...] = h1
        xn_ref[...] = _rms(h1, gf_ref[...]).astype(xn_ref.dtype)

    xn = xn_ref[...]
    sub = wup_ref.shape[1] // n_sub
    ffn = None
    for s in range(n_sub):
        f = _mm(xn, wup_ref[:, s * sub:(s + 1) * sub])
        part = _mm(jnp.square(jnp.maximum(f, 0.0)), wdn_ref[s * sub:(s + 1) * sub, :])
        ffn = part if ffn is None else ffn + part
    acc_ref[...] += ffn

    @pl.when(c == pl.num_programs(1) - 1)
    def _():
        h2 = acc_ref[...]
        gate = jax.nn.sigmoid(_mm(_rms(h2, gp_ref[...]), wg_ref[...]))
        h3 = h2 + _mm(p_ref[...], wp_ref[...]) * gate
        o_ref[...] = _rms(h3, gfin_ref[...]) if final else h3


def _tail(h, att, ssm, p, lw, g_final, tm, ff_chunks, n_sub, final, time_major_batch=None):
    t, d = h.shape
    aw, sw, pd = att.shape[1], lw["w_out_s"].shape[0], p.shape[1]
    d_ff = lw["w_up"].shape[1]
    ffc = d_ff // ff_chunks
    nt = t // tm
    wdt = lw["w_up"].dtype
    if time_major_batch is None:
        ssm_spec = pl.BlockSpec((tm, sw), lambda i, c: (i, 0))
    else:
        nlt = nt // time_major_batch
        ssm_spec = pl.BlockSpec((tm, sw), lambda i, c: (i % nlt, i // nlt))
    tok = lambda w: pl.BlockSpec((tm, w), lambda i, c: (i, 0))
    if ff_chunks == 1:
        up_spec, dn_spec = _const_spec((d, d_ff)), _const_spec((d_ff, d))
    else:
        up_spec = pl.BlockSpec((d, ffc), lambda i, c: (0, c))
        dn_spec = pl.BlockSpec((ffc, d), lambda i, c: (c, 0))
    vec = lambda g: g.reshape(1, -1)
    return pl.pallas_call(
        functools.partial(_tail_kernel, n_sub=n_sub, final=final),
        grid=(nt, ff_chunks),
        in_specs=[tok(d), tok(aw), ssm_spec, tok(pd),
                  _const_spec((1, aw)), _const_spec((1, sw)), _const_spec((aw, d)), _const_spec((sw, d)),
                  _const_spec((1, d)), up_spec, dn_spec,
                  _const_spec((1, d)), _const_spec((d, d)), _const_spec((pd, d)), _const_spec((1, d))],
        out_specs=tok(d),
        out_shape=jax.ShapeDtypeStruct((t, d), F32),
        scratch_shapes=[pltpu.VMEM((tm, d), F32), pltpu.VMEM((tm, d), wdt)],
        compiler_params=_params("parallel", "arbitrary"),
        name="tail",
    )(h, att, ssm, p, vec(lw["g_att_out"]), vec(lw["g_ssm_out"]), lw["w_out_a"], lw["w_out_s"],
      vec(lw["g_ffn"]), lw["w_up"], lw["w_down"], vec(lw["g_ple"]), lw["w_ple_gate"], lw["w_ple"], vec(g_final))


PROMPT_TILE = 512
SSM_STEPS = 64
FF_SUB = 1024


def kernel(x_prompt, x_sample, p_prompt, p_sample, cache_k, cache_v, state_ssm_re, state_ssm_im, page_table, w_in, w_out, g_mix, g_att_out, g_ssm_out, att_bias, ssm_a_re, ssm_a_im, ssm_log_dt, ssm_b_re, ssm_b_im, ssm_c_re, ssm_c_im, ssm_d, w_glu, g_ffn, w_up, w_down, w_ple, w_ple_gate, g_ple, g_final):
    b, seq, d = x_prompt.shape
    sb, st = x_sample.shape[0], x_sample.shape[1]
    assert st == 1 and b % SUBLANES == 0
    depth, n_groups, n_state = ssm_a_re.shape
    n_heads = att_bias.shape[1]
    hd = cache_k.shape[-1]
    aw = n_heads * hd
    sw = n_groups * ssm_b_re.shape[-1]
    sp = n_groups * n_state
    d_ff = w_up.shape[2]
    tm = min(PROMPT_TILE, seq)
    steps = min(SSM_STEPS, seq)
    n_sub = max(1, d_ff // FF_SUB)

    bbd, cbd, ab = _ssm_prep(ssm_a_re, ssm_a_im, ssm_log_dt, ssm_b_re, ssm_b_im, ssm_c_re, ssm_c_im)
    u2 = _suffix_matrix()
    pages_t = lambda c: c.transpose(0, 1, 3, 4, 2).reshape(depth, c.shape[1], aw, c.shape[2])
    cache_k, cache_v = pages_t(cache_k), pages_t(cache_v)

    def layer_weights(l, dt, kv_transposed):
        cast = lambda w: w[l].astype(dt)
        kv = (lambda w: cast(w).T) if kv_transposed else cast
        return {"w_q": cast(w_in[:, :, :aw]), "w_k": kv(w_in[:, :, aw:2 * aw]), "w_v": kv(w_in[:, :, 2 * aw:3 * aw]),
                "w_u": cast(w_in[:, :, 3 * aw:]), "w_out_a": cast(w_out[:, :aw]), "w_out_s": cast(w_out[:, aw:]),
                "w_up": cast(w_up), "w_down": cast(w_down), "w_ple": cast(w_ple),
                "w_ple_gate": cast(w_ple_gate), "w_glu": cast(w_glu), "bbd": cast(bbd), "cbd": cast(cbd),
                "g_att_out": g_att_out[l], "g_ssm_out": g_ssm_out[l], "g_ffn": g_ffn[l], "g_ple": g_ple[l]}

    hp = x_prompt.reshape(b * seq, d)
    hs = x_sample.reshape(sb, d)
    zero_state = jnp.zeros((b, 2 * sp), F32)
    outs = {name: [] for name in ("kp", "vp", "rp", "ip", "ks", "vs", "rs", "is")}
    for l in range(depth):
        final = l == depth - 1
        lw = layer_weights(l, BF16, True)
        q, k, v, u = _in_proj(hp, g_mix[l], lw, tm, prompt_batch=b)
        att = _attn_prompt(q.reshape(b, seq, aw), k, v, att_bias[l], u2, hd)
        y, h_last = _ssm(u.reshape(seq * b, sw), lw["bbd"], lw["cbd"], ab[l], ssm_d[l].reshape(-1),
                         lw["w_glu"], zero_state, rows=b, steps=steps)
        hp = _tail(hp, att.reshape(b * seq, aw), y.reshape(seq, b * sw), p_prompt[l].reshape(b * seq, -1),
                   lw, g_final, tm, 1, n_sub, final, time_major_batch=b)
        outs["kp"].append(k.reshape(b, n_heads, hd, seq))
        outs["vp"].append(v.reshape(b, n_heads, hd, seq))
        outs["rp"].append(h_last[:, :sp].reshape(b, n_groups, n_state))
        outs["ip"].append(h_last[:, sp:].reshape(b, n_groups, n_state))

        lw = layer_weights(l, F32, False)
        q, k, v, u = _in_proj(hs, g_mix[l], lw, sb)
        att = _attn_decode(q, att_bias[l], cache_k, cache_v, page_table, l, u2, hd)
        h0 = jnp.concatenate([state_ssm_re[l].reshape(sb, sp), state_ssm_im[l].reshape(sb, sp)], axis=1)
        y, h_last = _ssm(u, lw["bbd"], lw["cbd"], ab[l], ssm_d[l].reshape(-1), lw["w_glu"], h0,
                         rows=sb, steps=1)
        hs = _tail(hs, att, y, p_sample[l].reshape(sb, -1), lw, g_final, sb, n_sub, 1, final)
        outs["ks"].append(k.reshape(sb, 1, n_heads, hd))
        outs["vs"].append(v.reshape(sb, 1, n_heads, hd))
        outs["rs"].append(h_last[:, :sp].reshape(sb, n_groups, n_state))
        outs["is"].append(h_last[:, sp:].reshape(sb, n_groups, n_state))

    stack = lambda name: jnp.stack(outs[name])
    seq_major = lambda x: x.transpose(0, 1, 4, 2, 3)
    return (hp.reshape(b, seq, d), hs.reshape(sb, st, d),
            seq_major(stack("kp")), seq_major(stack("vp")), stack("rp"), stack("ip"),
            stack("ks"), stack("vs"), stack("rs"), stack("is"))
```

```python
import functools

import jax
import jax.numpy as jnp
from jax import lax
from jax.experimental import pallas as pl
from jax.experimental.pallas import tpu as pltpu

RMS_EPS = 1e-6
LANES = 128
SUBLANES = 8
VMEM_LIMIT = 56 * 1024 * 1024
BF16 = jnp.bfloat16
F32 = jnp.float32
_NT = (((1,), (1,)), ((), ()))
LOG2E = 1.4426950408889634
ATTN_GROUP = 4
ATTN_Q_ROWS = 256
ATTN_HEADS = 2
DECODE_PAGES = 16


def _mm(a, w):
    if w.dtype == BF16:
        return jnp.dot(a.astype(BF16), w, preferred_element_type=F32)
    return jnp.dot(a, w, preferred_element_type=F32, precision=lax.Precision.HIGHEST)


def _rms(x, g):
    return x * lax.rsqrt(jnp.mean(x * x, axis=-1, keepdims=True) + RMS_EPS) * g


def _const_spec(shape):
    nd = len(shape)
    return pl.BlockSpec(shape, lambda *_: (0,) * nd, pipeline_mode=pl.Buffered(1))


def _params(*sem):
    return pltpu.CompilerParams(dimension_semantics=sem, vmem_limit_bytes=VMEM_LIMIT)


def _ssm_prep_kernel(are_ref, aim_ref, ldt_ref, bre_ref, bim_ref, cre_ref, cim_ref,
                     bbd_ref, cbd_ref, ab_ref, *, c_shift, p_shift):
    a_re = are_ref[0]
    a_im = aim_ref[0]
    dt = jnp.exp(ldt_ref[0])
    mag = jnp.exp(a_re * dt)
    ab_re = mag * jnp.cos(a_im * dt)
    ab_im = mag * jnp.sin(a_im * dt)
    den = a_re * a_re + a_im * a_im
    f_re = ((ab_re - 1.0) * a_re + ab_im * a_im) / den
    f_im = (ab_im * a_re - (ab_re - 1.0) * a_im) / den
    cb, sp = bre_ref.shape[1], bre_ref.shape[2]
    ch0 = pl.program_id(1) * cb
    same_bc = ((lax.broadcasted_iota(jnp.int32, (cb, sp), 0) + ch0) >> c_shift) == \
              (lax.broadcasted_iota(jnp.int32, (cb, sp), 1) >> p_shift)
    br = bre_ref[0]
    bi = bim_ref[0]
    bbd_ref[0, :, :sp] = jnp.where(same_bc, f_re * br - f_im * bi, 0.0)
    bbd_ref[0, :, sp:] = jnp.where(same_bc, f_re * bi + f_im * br, 0.0)
    same_cb = (lax.broadcasted_iota(jnp.int32, (sp, cb), 0) >> p_shift) == \
              ((lax.broadcasted_iota(jnp.int32, (sp, cb), 1) + ch0) >> c_shift)
    cbd_ref[0, :sp, :] = jnp.where(same_cb, cre_ref[0], 0.0)
    cbd_ref[0, sp:, :] = jnp.where(same_cb, -cim_ref[0], 0.0)
    ab_ref[0, :, :sp] = jnp.broadcast_to(ab_re, (SUBLANES, sp))
    ab_ref[0, :, sp:] = jnp.broadcast_to(ab_im, (SUBLANES, sp))


def _ssm_prep(a_re, a_im, log_dt, b_re, b_im, c_re, c_im):
    depth, g, p = a_re.shape
    c = b_re.shape[-1]
    sw, sp = g * c, g * p
    assert c & (c - 1) == 0 and p & (p - 1) == 0 and sw % LANES == 0
    flat = lambda x: x.reshape(depth, 1, sp)
    ldt = jnp.repeat(log_dt, p, axis=1).reshape(depth, 1, sp)
    tile_b = lambda b: jnp.tile(b.transpose(0, 1, 3, 2).reshape(depth, sw, p), (1, 1, g))
    tile_c = lambda x: jnp.tile(x.transpose(0, 3, 1, 2).reshape(depth, p, sw), (1, g, 1))
    vec = pl.BlockSpec((1, 1, sp), lambda l, j: (l, 0, 0))
    bspec = pl.BlockSpec((1, LANES, sp), lambda l, j: (l, j, 0))
    cspec = pl.BlockSpec((1, sp, LANES), lambda l, j: (l, 0, j))
    return pl.pallas_call(
        functools.partial(_ssm_prep_kernel, c_shift=c.bit_length() - 1, p_shift=p.bit_length() - 1),
        grid=(depth, sw // LANES),
        in_specs=[vec, vec, vec, bspec, bspec, cspec, cspec],
        out_specs=[pl.BlockSpec((1, LANES, 2 * sp), lambda l, j: (l, j, 0)),
                   pl.BlockSpec((1, 2 * sp, LANES), lambda l, j: (l, 0, j)),
                   pl.BlockSpec((1, SUBLANES, 2 * sp), lambda l, j: (l, 0, 0))],
        out_shape=[jax.ShapeDtypeStruct((depth, sw, 2 * sp), F32),
                   jax.ShapeDtypeStruct((depth, 2 * sp, sw), F32),
                   jax.ShapeDtypeStruct((depth, SUBLANES, 2 * sp), F32)],
        compiler_params=_params("arbitrary", "arbitrary"),
        name="ssm_prep",
    )(flat(a_re), flat(a_im), ldt, tile_b(b_re), tile_b(b_im), tile_c(c_re), tile_c(c_im))


def _in_proj_kernel(x_ref, g_ref, wq_ref, wk_ref, wv_ref, wu_ref, *rest, kv_transposed):
    q_ref, k_ref, v_ref, u_ref = rest[-4:]
    xn = _rms(x_ref[...], g_ref[...])
    if wq_ref.dtype == BF16:
        xn = xn.astype(BF16)
    q_ref[...] = _mm(xn, wq_ref[...])
    u_ref[...] = _mm(xn, wu_ref[...])
    if kv_transposed:
        k_ref[0, 0] = lax.dot_general(wk_ref[...], xn, _NT, preferred_element_type=F32)
        v_ref[0, 0] = lax.dot_general(wv_ref[...], xn, _NT, preferred_element_type=F32)
    else:
        k_ref[...] = _mm(xn, wk_ref[...])
        v_ref[...] = _mm(xn, wv_ref[...])


def _in_proj(x, g, lw, tm, prompt_batch=None, layer=None, depth=None, kv_all=None):
    t, d = x.shape
    aw, sw = lw["w_q"].shape[1], lw["w_u"].shape[1]
    nt = t // tm
    row = pl.BlockSpec((tm, aw), lambda i: (i, 0))
    extra_in, extra_specs, aliases = [], [], {}
    if prompt_batch is None:
        kv_spec, kv_shape = row, (t, aw)
        u_spec, u_shape = pl.BlockSpec((tm, sw), lambda i: (i, 0)), (t, sw)
    else:
        nlt = nt // prompt_batch
        kv_spec = pl.BlockSpec((1, 1, aw, tm), lambda i: (layer, i // nlt, 0, i % nlt))
        kv_shape = (depth, prompt_batch, aw, t // prompt_batch)
        u_spec = pl.BlockSpec((tm, sw), lambda i: (i % nlt, i // nlt))
        u_shape = (t // prompt_batch, prompt_batch * sw)
        extra_in = list(kv_all)
        extra_specs = [pl.BlockSpec(memory_space=pl.ANY)] * 2
        aliases = {6: 1, 7: 2}
    return pl.pallas_call(
        functools.partial(_in_proj_kernel, kv_transposed=prompt_batch is not None),
        grid=(nt,),
        in_specs=[pl.BlockSpec((tm, d), lambda i: (i, 0)), _const_spec((1, d)), _const_spec(lw["w_q"].shape),
                  _const_spec(lw["w_k"].shape), _const_spec(lw["w_v"].shape), _const_spec(lw["w_u"].shape)]
                 + extra_specs,
        out_specs=[row, kv_spec, kv_spec, u_spec],
        out_shape=[jax.ShapeDtypeStruct((t, aw), F32), jax.ShapeDtypeStruct(kv_shape, F32),
                   jax.ShapeDtypeStruct(kv_shape, F32), jax.ShapeDtypeStruct(u_shape, F32)],
        input_output_aliases=aliases,
        compiler_params=_params("parallel"),
        name="in_proj",
    )(x, g.reshape(1, d), lw["w_q"], lw["w_k"], lw["w_v"], lw["w_u"], *extra_in)


def _stick_terms(z2):
    sp = jnp.maximum(z2, 0.0) + jnp.log2(1.0 + jnp.exp2(-jnp.abs(z2)))
    return sp, z2 - sp


def _suffix_sums(sp, u2_ref):
    return jnp.dot(sp.astype(BF16), u2_ref[...], preferred_element_type=F32)


def _suffix_matrix():
    j = lax.broadcasted_iota(jnp.int32, (LANES, 2 * LANES), 0)
    s = lax.broadcasted_iota(jnp.int32, (LANES, 2 * LANES), 1)
    return jnp.where((s >= LANES) | (j > s), 1.0, 0.0).astype(BF16)


def _attn_prompt_kernel(bias_ref, q_ref, k_ref, v_ref, u2_ref, o_ref, kb_ref, vb_ref, acc_ref, car_ref, *, hd):
    nh, tq = acc_ref.shape[0], acc_ref.shape[1]
    qb = tq // LANES
    seq, lw = q_ref.shape[1], q_ref.shape[2]
    sup = kb_ref.shape[2] // LANES
    tk = sup * LANES
    sup_shift = sup.bit_length() - 1
    hp = pl.program_id(1)
    for s in range(seq // tk):
        cols = slice(s * tk, (s + 1) * tk)
        kb_ref[s] = k_ref[0, 0, :, cols].astype(BF16)
        vb_ref[s] = v_ref[0, 0, :, cols].T.astype(BF16)
    head_of_lane = lax.shift_right_logical(lax.broadcasted_iota(jnp.int32, (tq, lw), 1), hd.bit_length() - 1)
    first_of_pair = lax.broadcasted_iota(jnp.int32, (tq, LANES), 1) < hd
    causal = lax.broadcasted_iota(jnp.int32, (tq, tq), 1) < lax.broadcasted_iota(jnp.int32, (tq, tq), 0)
    scale = hd ** -0.5 * LOG2E
    bias = [bias_ref[nh * hp + h] * LOG2E for h in range(nh)]
    heads = range(nh)
    pair_lanes = lambda h: slice((h // 2) * LANES, (h // 2 + 1) * LANES)

    def group(qms, s, nb, diag):
        wd = nb * LANES
        blocks = lambda x: [x[:, k * LANES:(k + 1) * LANES] for k in range(nb)]
        own = lambda k: causal[:, (k - (nb - qb)) * LANES:(k - (nb - qb) + 1) * LANES]
        z = [jnp.dot(qms[h], kb_ref[s, :, :wd], preferred_element_type=F32) + bias[h] for h in heads]
        terms = [_stick_terms(z[h]) for h in heads]
        sp = [blocks(terms[h][0]) for h in heads]
        ls = [blocks(terms[h][1]) for h in heads]
        if diag:
            for h in heads:
                for k in range(nb - qb, nb):
                    sp[h][k] = jnp.where(own(k), sp[h][k], 0.0)
        c = [_suffix_sums(jnp.concatenate(sp[h], axis=0), u2_ref) for h in heads]
        w = []
        for h in heads:
            carry = None if diag else car_ref[h]
            wh = [None] * nb
            for k in reversed(range(nb)):
                rows = slice(k * tq, (k + 1) * tq)
                t = ls[h][k] - c[h][rows, :LANES]
                if carry is not None:
                    t = t - carry
                e = jnp.exp2(t)
                if diag and k >= nb - qb:
                    e = jnp.where(own(k), e, 0.0)
                wh[k] = e.astype(BF16)
                tot = c[h][rows, LANES:]
                carry = tot if carry is None else carry + tot
            car_ref[h] = carry
            w.append(jnp.concatenate(wh, axis=1))
        pv = [jnp.dot(w[h], vb_ref[s, :wd, pair_lanes(h)], preferred_element_type=F32) for h in heads]
        for h in heads:
            acc_ref[h] = pv[h] if diag else acc_ref[h] + pv[h]

    def q_block(i, _):
        rows = pl.ds(pl.multiple_of(i * tq, tq), tq)
        q = q_ref[0, rows, :] * scale
        qms = [jnp.where(head_of_lane == h, q, 0.0).astype(BF16) for h in heads]
        last = i * qb + (qb - 1)
        s_own = lax.shift_right_logical(last, sup_shift)
        for nb in range(qb, sup + 1, qb):
            @pl.when((last & (sup - 1)) == nb - 1)
            def _(nb=nb):
                group(qms, s_own, nb, True)

        def older(ss, _):
            group(qms, s_own - 1 - ss, sup, False)
            return 0

        lax.fori_loop(0, s_own, older, 0)
        o_ref[0, rows, :] = jnp.concatenate(
            [jnp.where(first_of_pair, acc_ref[h], acc_ref[h + 1]) for h in range(0, nh, 2)], axis=1)
        return 0

    lax.fori_loop(0, seq // tq, q_block, 0)


def _attn_prompt(q, k_t, v_t, layer, bias, u2, hd):
    b, seq, aw = q.shape
    assert 2 * hd == LANES and seq % LANES == 0 and aw % LANES == 0
    sup = max(s for s in (1, 2, ATTN_GROUP) if seq % (s * LANES) == 0 and s <= ATTN_GROUP)
    tk = sup * LANES
    tq = ATTN_Q_ROWS if tk % ATTN_Q_ROWS == 0 else LANES
    lw = ATTN_HEADS * hd if aw % (ATTN_HEADS * hd) == 0 else LANES
    nh = lw // hd
    blk = pl.BlockSpec((1, seq, lw), lambda bi, hp: (bi, 0, hp))
    blk_t = pl.BlockSpec((1, 1, lw, seq), lambda bi, hp: (layer, bi, hp, 0))
    return pl.pallas_call(
        functools.partial(_attn_prompt_kernel, hd=hd),
        grid=(b, aw // lw),
        in_specs=[pl.BlockSpec(memory_space=pltpu.SMEM), blk, blk_t, blk_t, _const_spec((LANES, 2 * LANES))],
        out_specs=blk,
        out_shape=jax.ShapeDtypeStruct((b, seq, aw), F32),
        scratch_shapes=[pltpu.VMEM((seq // tk, lw, tk), BF16), pltpu.VMEM((seq // tk, tk, lw), BF16),
                        pltpu.VMEM((nh, tq, LANES), F32), pltpu.VMEM((nh, tq, LANES), F32)],
        compiler_params=_params("parallel", "parallel"),
        name="attn_prompt",
    )(bias, q, k_t, v_t, u2)


def _attn_decode_kernel(pt_ref, q_ref, bias_ref, *refs, hd, npp):
    del pt_ref
    k_refs, v_refs = refs[:npp], refs[npp:2 * npp]
    u2_ref, o_ref, qbd_ref, acc_ref, car_ref = refs[2 * npp:]
    j = pl.program_id(1)
    rows, aw = qbd_ref.shape
    own = (lax.broadcasted_iota(jnp.int32, (rows, aw), 1) // hd) == lax.broadcasted_iota(jnp.int32, (rows, aw), 0)

    @pl.when(j == 0)
    def _():
        q = jnp.broadcast_to(q_ref[0] * (hd ** -0.5 * LOG2E), (rows, aw))
        qbd_ref[...] = jnp.where(own, q, 0.0).astype(BF16)
        acc_ref[...] = jnp.zeros_like(acc_ref)
        car_ref[...] = jnp.zeros_like(car_ref)

    qbd = qbd_ref[...]
    bias2 = bias_ref[...] * LOG2E
    terms = [_stick_terms(jnp.dot(qbd, k_refs[p][0, 0].astype(BF16), preferred_element_type=F32) + bias2)
             for p in range(npp)]
    c = _suffix_sums(jnp.concatenate([t[0] for t in terms], axis=0), u2_ref)
    carry = car_ref[...]
    pv = None
    for p in range(npp):
        r = slice(p * rows, (p + 1) * rows)
        w = jnp.exp2(terms[p][1] - c[r, :LANES] - carry).astype(BF16)
        carry = carry + c[r, LANES:]
        part = lax.dot_general(w, v_refs[p][0, 0].astype(BF16), _NT, preferred_element_type=F32)
        pv = part if pv is None else pv + part
    car_ref[...] = carry
    acc_ref[...] += pv

    @pl.when(j == pl.num_programs(1) - 1)
    def _():
        o_ref[0] = jnp.sum(jnp.where(own, acc_ref[...], 0.0), axis=0, keepdims=True)


def _attn_decode(q, bias, cache_k, cache_v, page_table, layer, u2, hd):
    s, aw = q.shape
    n_pages = page_table.shape[1]
    page = cache_k.shape[3]
    nh = aw // hd
    rows = 2 * SUBLANES
    assert page == LANES and nh <= rows
    bias_rows = jnp.broadcast_to(jnp.pad(bias, (0, rows - nh))[:, None], (rows, LANES))
    npp = max(n for n in range(1, DECODE_PAGES + 1) if n_pages % n == 0)

    def kv_spec(p):
        return pl.BlockSpec((1, 1, aw, page), lambda b, j, pt: (layer, pt[b, n_pages - 1 - (j * npp + p)], 0, 0))

    q_spec = pl.BlockSpec((1, 1, aw), lambda b, j, pt: (b, 0, 0))
    pages = [kv_spec(p) for p in range(npp)]
    out = pl.pallas_call(
        functools.partial(_attn_decode_kernel, hd=hd, npp=npp),
        grid_spec=pltpu.PrefetchScalarGridSpec(
            num_scalar_prefetch=1,
            grid=(s, n_pages // npp),
            in_specs=[q_spec, pl.BlockSpec((rows, LANES), lambda b, j, pt: (0, 0))] + pages + pages +
                     [pl.BlockSpec((LANES, 2 * LANES), lambda b, j, pt: (0, 0))],
            out_specs=q_spec,
            scratch_shapes=[pltpu.VMEM((rows, aw), BF16), pltpu.VMEM((rows, aw), F32),
                            pltpu.VMEM((rows, LANES), F32)],
        ),
        out_shape=jax.ShapeDtypeStruct((s, 1, aw), F32),
        compiler_params=_params("parallel", "arbitrary"),
        name="attn_decode",
    )(page_table, q.reshape(s, 1, aw), bias_rows, *([cache_k] * npp), *([cache_v] * npp), u2)
    return out.reshape(s, aw)


def _ssm_kernel(u_ref, bbd_ref, cbd_ref, ab_ref, d_ref, wglu_ref, h0_ref, y_ref, ht_ref, bu_ref, h_ref,
                *, rows, steps, cw):
    c = pl.program_id(0)
    sp = h_ref.shape[1] // 2

    @pl.when(c == 0)
    def _():
        h_ref[...] = h0_ref[...]

    sw = u_ref.shape[1]
    halves = [(slice(k * (sw // 2), (k + 1) * (sw // 2)), k * (sp // 2), (k + 1) * (sp // 2)) for k in range(2)]
    u = u_ref[...]
    for ch, lo, hi in halves:
        for off in (0, sp):
            bu_ref[:, off + lo:off + hi] = _mm(u[:, ch], bbd_ref[ch, off + lo:off + hi])
    for cc in range(sp // cw):
        re = slice(cc * cw, (cc + 1) * cw)
        im = slice(sp + cc * cw, sp + (cc + 1) * cw)
        a_r = jnp.broadcast_to(ab_ref[0:1, re], (rows, cw))
        a_i = jnp.broadcast_to(ab_ref[0:1, im], (rows, cw))

        def step(t, carry):
            h_r, h_i = carry
            r = pl.ds(pl.multiple_of(t * rows, rows), rows)
            n_r = a_r * h_r - a_i * h_i + bu_ref[r, re]
            n_i = a_r * h_i + a_i * h_r + bu_ref[r, im]
            bu_ref[r, re] = n_r
            bu_ref[r, im] = n_i
            return n_r, n_i

        h_r, h_i = lax.fori_loop(0, steps, step, (h_ref[:, re], h_ref[:, im]), unroll=min(steps, 8))
        h_ref[:, re] = h_r
        h_ref[:, im] = h_i

    y = jnp.concatenate(
        [_mm(bu_ref[:, lo:hi], cbd_ref[lo:hi, ch]) + _mm(bu_ref[:, sp + lo:sp + hi], cbd_ref[sp + lo:sp + hi, ch])
         for ch, lo, hi in halves], axis=1) + d_ref[...] * u
    y = jax.nn.gelu(y, approximate=True)
    y_ref[...] = y * jax.nn.sigmoid(_mm(y, wglu_ref[...]))

    @pl.when(c == pl.num_programs(0) - 1)
    def _():
        ht_ref[...] = h_ref[...]


def _ssm(u, bbd, cbd, ab, d_skip, w_glu, h0, rows, steps):
    n, sw = u.shape
    sp2 = bbd.shape[1]
    chunk = rows * steps
    cw = 4 * LANES * SUBLANES // rows if rows <= 4 * SUBLANES else LANES
    cw = max(LANES, min(cw, sp2 // 2))
    blk = pl.BlockSpec((chunk, sw), lambda c: (c, 0))
    return pl.pallas_call(
        functools.partial(_ssm_kernel, rows=rows, steps=steps, cw=cw),
        grid=(n // chunk,),
        in_specs=[blk, _const_spec((sw, sp2)), _const_spec((sp2, sw)), _const_spec((SUBLANES, sp2)),
                  _const_spec((1, sw)), _const_spec((sw, sw)), _const_spec((rows, sp2))],
        out_specs=[blk, pl.BlockSpec((rows, sp2), lambda c: (0, 0))],
        out_shape=[jax.ShapeDtypeStruct((n, sw), F32), jax.ShapeDtypeStruct((rows, sp2), F32)],
        scratch_shapes=[pltpu.VMEM((chunk, sp2), F32), pltpu.VMEM((rows, sp2), F32)],
        compiler_params=_params("arbitrary"),
        name="ssm",
    )(u, bbd, cbd, ab, d_skip.reshape(1, sw), w_glu, h0)


def _tail_kernel(h_ref, att_ref, ssm_ref, p_ref, ga_ref, gs_ref, woa_ref, wos_ref, gf_ref, wup_ref, wdn_ref,
                 gp_ref, wg_ref, wp_ref, gfin_ref, o_ref, acc_ref, xn_ref, *, n_sub, final):
    c = pl.program_id(1)

    @pl.when(c == 0)
    def _():
        na = _rms(att_ref[...], ga_ref[...])
        ns = _rms(ssm_ref[...], gs_ref[...])
        h1 = h_ref[...] + _mm(na, woa_ref[...]) + _mm(ns, wos_ref[...])
        acc_ref[...] = h1
        xn_ref[...] = _rms(h1, gf_ref[...]).astype(xn_ref.dtype)

    xn = xn_ref[...]
    sub = wup_ref.shape[1] // n_sub
    ffn = None
    for s in range(n_sub):
        f = _mm(xn, wup_ref[:, s * sub:(s + 1) * sub])
        part = _mm(jnp.square(jnp.maximum(f, 0.0)), wdn_ref[s * sub:(s + 1) * sub, :])
        ffn = part if ffn is None else ffn + part
    acc_ref[...] += ffn

    @pl.when(c == pl.num_programs(1) - 1)
    def _():
        h2 = acc_ref[...]
        gate = jax.nn.sigmoid(_mm(_rms(h2, gp_ref[...]), wg_ref[...]))
        h3 = h2 + _mm(p_ref[...], wp_ref[...]) * gate
        o_ref[...] = _rms(h3, gfin_ref[...]) if final else h3


def _tail(h, att, ssm, p, lw, g_final, tm, ff_chunks, n_sub, final, time_major_batch=None):
    t, d = h.shape
    aw, sw, pd = att.shape[1], lw["w_out_s"].shape[0], p.shape[1]
    d_ff = lw["w_up"].shape[1]
    ffc = d_ff // ff_chunks
    nt = t // tm
    wdt = lw["w_up"].dtype
    if time_major_batch is None:
        ssm_spec = pl.BlockSpec((tm, sw), lambda i, c: (i, 0))
    else:
        nlt = nt // time_major_batch
        ssm_spec = pl.BlockSpec((tm, sw), lambda i, c: (i % nlt, i // nlt))
    tok = lambda w: pl.BlockSpec((tm, w), lambda i, c: (i, 0))
    if ff_chunks == 1:
        up_spec, dn_spec = _const_spec((d, d_ff)), _const_spec((d_ff, d))
    else:
        up_spec = pl.BlockSpec((d, ffc), lambda i, c: (0, c))
        dn_spec = pl.BlockSpec((ffc, d), lambda i, c: (c, 0))
    vec = lambda g: g.reshape(1, -1)
    return pl.pallas_call(
        functools.partial(_tail_kernel, n_sub=n_sub, final=final),
        grid=(nt, ff_chunks),
        in_specs=[tok(d), tok(aw), ssm_spec, tok(pd),
                  _const_spec((1, aw)), _const_spec((1, sw)), _const_spec((aw, d)), _const_spec((sw, d)),
                  _const_spec((1, d)), up_spec, dn_spec,
                  _const_spec((1, d)), _const_spec((d, d)), _const_spec((pd, d)), _const_spec((1, d))],
        out_specs=tok(d),
        out_shape=jax.ShapeDtypeStruct((t, d), F32),
        scratch_shapes=[pltpu.VMEM((tm, d), F32), pltpu.VMEM((tm, d), wdt)],
        compiler_params=_params("parallel", "arbitrary"),
        name="tail",
    )(h, att, ssm, p, vec(lw["g_att_out"]), vec(lw["g_ssm_out"]), lw["w_out_a"], lw["w_out_s"],
      vec(lw["g_ffn"]), lw["w_up"], lw["w_down"], vec(lw["g_ple"]), lw["w_ple_gate"], lw["w_ple"], vec(g_final))


PROMPT_TILE = 512
SSM_STEPS = 64
FF_SUB = 1024


def kernel(x_prompt, x_sample, p_prompt, p_sample, cache_k, cache_v, state_ssm_re, state_ssm_im, page_table, w_in, w_out, g_mix, g_att_out, g_ssm_out, att_bias, ssm_a_re, ssm_a_im, ssm_log_dt, ssm_b_re, ssm_b_im, ssm_c_re, ssm_c_im, ssm_d, w_glu, g_ffn, w_up, w_down, w_ple, w_ple_gate, g_ple, g_final):
    b, seq, d = x_prompt.shape
    sb, st = x_sample.shape[0], x_sample.shape[1]
    assert st == 1 and b % SUBLANES == 0
    depth, n_groups, n_state = ssm_a_re.shape
    n_heads = att_bias.shape[1]
    hd = cache_k.shape[-1]
    aw = n_heads * hd
    sw = n_groups * ssm_b_re.shape[-1]
    sp = n_groups * n_state
    d_ff = w_up.shape[2]
    tm = min(PROMPT_TILE, seq)
    steps = min(SSM_STEPS, seq)
    n_sub = max(1, d_ff // FF_SUB)

    bbd, cbd, ab = _ssm_prep(ssm_a_re, ssm_a_im, ssm_log_dt, ssm_b_re, ssm_b_im, ssm_c_re, ssm_c_im)
    u2 = _suffix_matrix()
    pages_t = lambda c: c.transpose(0, 1, 3, 4, 2).reshape(depth, c.shape[1], aw, c.shape[2])
    cache_k, cache_v = pages_t(cache_k), pages_t(cache_v)

    def layer_weights(l, dt, kv_transposed):
        cast = lambda w: w[l].astype(dt)
        kv = (lambda w: cast(w).T) if kv_transposed else cast
        return {"w_q": cast(w_in[:, :, :aw]), "w_k": kv(w_in[:, :, aw:2 * aw]), "w_v": kv(w_in[:, :, 2 * aw:3 * aw]),
                "w_u": cast(w_in[:, :, 3 * aw:]), "w_out_a": cast(w_out[:, :aw]), "w_out_s": cast(w_out[:, aw:]),
                "w_up": cast(w_up), "w_down": cast(w_down), "w_ple": cast(w_ple),
                "w_ple_gate": cast(w_ple_gate), "w_glu": cast(w_glu), "bbd": cast(bbd), "cbd": cast(cbd),
                "g_att_out": g_att_out[l], "g_ssm_out": g_ssm_out[l], "g_ffn": g_ffn[l], "g_ple": g_ple[l]}

    hp = x_prompt.reshape(b * seq, d)
    hs = x_sample.reshape(sb, d)
    zero_state = jnp.zeros((b, 2 * sp), F32)
    outs = {name: [] for name in ("rp", "ip", "ks", "vs", "rs", "is")}
    kv_all = (jnp.zeros((depth, b, aw, seq), F32), jnp.zeros((depth, b, aw, seq), F32))
    for l in range(depth):
        final = l == depth - 1
        lw = layer_weights(l, BF16, True)
        q, k_all, v_all, u = _in_proj(hp, g_mix[l], lw, tm, prompt_batch=b, layer=l, depth=depth, kv_all=kv_all)
        kv_all = (k_all, v_all)
        att = _attn_prompt(q.reshape(b, seq, aw), k_all, v_all, l, att_bias[l], u2, hd)
        y, h_last = _ssm(u.reshape(seq * b, sw), lw["bbd"], lw["cbd"], ab[l], ssm_d[l].reshape(-1),
                         lw["w_glu"], zero_state, rows=b, steps=steps)
        hp = _tail(hp, att.reshape(b * seq, aw), y.reshape(seq, b * sw), p_prompt[l].reshape(b * seq, -1),
                   lw, g_final, tm, 1, n_sub, final, time_major_batch=b)
        outs["rp"].append(h_last[:, :sp].reshape(b, n_groups, n_state))
        outs["ip"].append(h_last[:, sp:].reshape(b, n_groups, n_state))

        lw = layer_weights(l, BF16, False)
        q, k, v, u = _in_proj(hs, g_mix[l], lw, sb)
        att = _attn_decode(q, att_bias[l], cache_k, cache_v, page_table, l, u2, hd)
        h0 = jnp.concatenate([state_ssm_re[l].reshape(sb, sp), state_ssm_im[l].reshape(sb, sp)], axis=1)
        y, h_last = _ssm(u, lw["bbd"], lw["cbd"], ab[l], ssm_d[l].reshape(-1), lw["w_glu"], h0,
                         rows=sb, steps=1)
        hs = _tail(hs, att, y, p_sample[l].reshape(sb, -1), lw, g_final, sb, 1, n_sub, final)
        outs["ks"].append(k.reshape(sb, 1, n_heads, hd))
        outs["vs"].append(v.reshape(sb, 1, n_heads, hd))
        outs["rs"].append(h_last[:, :sp].reshape(sb, n_groups, n_state))
        outs["is"].append(h_last[:, sp:].reshape(sb, n_groups, n_state))

    stack = lambda name: jnp.stack(outs[name])
    seq_major = lambda x: x.reshape(depth, b, n_heads, hd, seq).transpose(0, 1, 4, 2, 3)
    return (hp.reshape(b, seq, d), hs.reshape(sb, st, d),
            seq_major(kv_all[0]), seq_major(kv_all[1]), stack("rp"), stack("ip"),
            stack("ks"), stack("vs"), stack("rs"), stack("is"))
```

```python
import functools

import jax
import jax.numpy as jnp
from jax import lax
from jax.experimental import pallas as pl
from jax.experimental.pallas import tpu as pltpu

RMS_EPS = 1e-6
LANES = 128
SUBLANES = 8
VMEM_LIMIT = 56 * 1024 * 1024
BF16 = jnp.bfloat16
F32 = jnp.float32
_NT = (((1,), (1,)), ((), ()))
LOG2E = 1.4426950408889634
ATTN_GROUP = 4
ATTN_Q_ROWS = 256
ATTN_HEADS = 2
DECODE_PAGES = 16


def _mm(a, w):
    if w.dtype == BF16:
        return jnp.dot(a.astype(BF16), w, preferred_element_type=F32)
    return jnp.dot(a, w, preferred_element_type=F32, precision=lax.Precision.HIGHEST)


def _rms(x, g):
    return x * lax.rsqrt(jnp.mean(x * x, axis=-1, keepdims=True) + RMS_EPS) * g


def _const_spec(shape):
    nd = len(shape)
    return pl.BlockSpec(shape, lambda *_: (0,) * nd, pipeline_mode=pl.Buffered(1))


def _params(*sem):
    return pltpu.CompilerParams(dimension_semantics=sem, vmem_limit_bytes=VMEM_LIMIT)


def _ssm_prep_kernel(are_ref, aim_ref, ldt_ref, bre_ref, bim_ref, cre_ref, cim_ref,
                     bbd_ref, cbd_ref, ab_ref, *, c_shift, p_shift):
    a_re = are_ref[0]
    a_im = aim_ref[0]
    dt = jnp.exp(ldt_ref[0])
    mag = jnp.exp(a_re * dt)
    ab_re = mag * jnp.cos(a_im * dt)
    ab_im = mag * jnp.sin(a_im * dt)
    den = a_re * a_re + a_im * a_im
    f_re = ((ab_re - 1.0) * a_re + ab_im * a_im) / den
    f_im = (ab_im * a_re - (ab_re - 1.0) * a_im) / den
    cb, sp = bre_ref.shape[1], bre_ref.shape[2]
    ch0 = pl.program_id(1) * cb
    same_bc = ((lax.broadcasted_iota(jnp.int32, (cb, sp), 0) + ch0) >> c_shift) == \
              (lax.broadcasted_iota(jnp.int32, (cb, sp), 1) >> p_shift)
    br = bre_ref[0]
    bi = bim_ref[0]
    bbd_ref[0, :, :sp] = jnp.where(same_bc, f_re * br - f_im * bi, 0.0)
    bbd_ref[0, :, sp:] = jnp.where(same_bc, f_re * bi + f_im * br, 0.0)
    same_cb = (lax.broadcasted_iota(jnp.int32, (sp, cb), 0) >> p_shift) == \
              ((lax.broadcasted_iota(jnp.int32, (sp, cb), 1) + ch0) >> c_shift)
    cbd_ref[0, :sp, :] = jnp.where(same_cb, cre_ref[0], 0.0)
    cbd_ref[0, sp:, :] = jnp.where(same_cb, -cim_ref[0], 0.0)
    ab_ref[0, :, :sp] = jnp.broadcast_to(ab_re, (SUBLANES, sp))
    ab_ref[0, :, sp:] = jnp.broadcast_to(ab_im, (SUBLANES, sp))


def _ssm_prep(a_re, a_im, log_dt, b_re, b_im, c_re, c_im):
    depth, g, p = a_re.shape
    c = b_re.shape[-1]
    sw, sp = g * c, g * p
    assert c & (c - 1) == 0 and p & (p - 1) == 0 and sw % LANES == 0
    flat = lambda x: x.reshape(depth, 1, sp)
    ldt = jnp.repeat(log_dt, p, axis=1).reshape(depth, 1, sp)
    tile_b = lambda b: jnp.tile(b.transpose(0, 1, 3, 2).reshape(depth, sw, p), (1, 1, g))
    tile_c = lambda x: jnp.tile(x.transpose(0, 3, 1, 2).reshape(depth, p, sw), (1, g, 1))
    vec = pl.BlockSpec((1, 1, sp), lambda l, j: (l, 0, 0))
    bspec = pl.BlockSpec((1, LANES, sp), lambda l, j: (l, j, 0))
    cspec = pl.BlockSpec((1, sp, LANES), lambda l, j: (l, 0, j))
    return pl.pallas_call(
        functools.partial(_ssm_prep_kernel, c_shift=c.bit_length() - 1, p_shift=p.bit_length() - 1),
        grid=(depth, sw // LANES),
        in_specs=[vec, vec, vec, bspec, bspec, cspec, cspec],
        out_specs=[pl.BlockSpec((1, LANES, 2 * sp), lambda l, j: (l, j, 0)),
                   pl.BlockSpec((1, 2 * sp, LANES), lambda l, j: (l, 0, j)),
                   pl.BlockSpec((1, SUBLANES, 2 * sp), lambda l, j: (l, 0, 0))],
        out_shape=[jax.ShapeDtypeStruct((depth, sw, 2 * sp), F32),
                   jax.ShapeDtypeStruct((depth, 2 * sp, sw), F32),
                   jax.ShapeDtypeStruct((depth, SUBLANES, 2 * sp), F32)],
        compiler_params=_params("arbitrary", "arbitrary"),
        name="ssm_prep",
    )(flat(a_re), flat(a_im), ldt, tile_b(b_re), tile_b(b_im), tile_c(c_re), tile_c(c_im))


def _in_proj_kernel(x_ref, g_ref, wq_ref, wk_ref, wv_ref, wu_ref, *rest, kv_transposed):
    q_ref, k_ref, v_ref, u_ref = rest[-4:]
    xn = _rms(x_ref[...], g_ref[...])
    if wq_ref.dtype == BF16:
        xn = xn.astype(BF16)
    q_ref[...] = _mm(xn, wq_ref[...])
    u_ref[...] = _mm(xn, wu_ref[...])
    if kv_transposed:
        k_ref[0, 0] = lax.dot_general(wk_ref[...], xn, _NT, preferred_element_type=F32)
        v_ref[0, 0] = lax.dot_general(wv_ref[...], xn, _NT, preferred_element_type=F32)
    else:
        k_ref[...] = _mm(xn, wk_ref[...])
        v_ref[...] = _mm(xn, wv_ref[...])


def _in_proj(x, g, lw, tm, prompt_batch=None, layer=None, depth=None, kv_all=None):
    t, d = x.shape
    aw, sw = lw["w_q"].shape[1], lw["w_u"].shape[1]
    nt = t // tm
    row = pl.BlockSpec((tm, aw), lambda i: (i, 0))
    extra_in, extra_specs, aliases = [], [], {}
    if prompt_batch is None:
        kv_spec, kv_shape = row, (t, aw)
        u_spec, u_shape = pl.BlockSpec((tm, sw), lambda i: (i, 0)), (t, sw)
    else:
        nlt = nt // prompt_batch
        kv_spec = pl.BlockSpec((1, 1, aw, tm), lambda i: (layer, i // nlt, 0, i % nlt))
        kv_shape = (depth, prompt_batch, aw, t // prompt_batch)
        u_spec = pl.BlockSpec((tm, sw), lambda i: (i % nlt, i // nlt))
        u_shape = (t // prompt_batch, prompt_batch * sw)
        extra_in = list(kv_all)
        extra_specs = [pl.BlockSpec(memory_space=pl.ANY)] * 2
        aliases = {6: 1, 7: 2}
    return pl.pallas_call(
        functools.partial(_in_proj_kernel, kv_transposed=prompt_batch is not None),
        grid=(nt,),
        in_specs=[pl.BlockSpec((tm, d), lambda i: (i, 0)), _const_spec((1, d)), _const_spec(lw["w_q"].shape),
                  _const_spec(lw["w_k"].shape), _const_spec(lw["w_v"].shape), _const_spec(lw["w_u"].shape)]
                 + extra_specs,
        out_specs=[row, kv_spec, kv_spec, u_spec],
        out_shape=[jax.ShapeDtypeStruct((t, aw), F32), jax.ShapeDtypeStruct(kv_shape, F32),
                   jax.ShapeDtypeStruct(kv_shape, F32), jax.ShapeDtypeStruct(u_shape, F32)],
        input_output_aliases=aliases,
        compiler_params=_params("parallel"),
        name="in_proj",
    )(x, g.reshape(1, d), lw["w_q"], lw["w_k"], lw["w_v"], lw["w_u"], *extra_in)


def _stick_terms(z2):
    sp = jnp.maximum(z2, 0.0) + jnp.log2(1.0 + jnp.exp2(-jnp.abs(z2)))
    return sp, z2 - sp


def _suffix_sums(sp, u2_ref):
    return jnp.dot(sp.astype(BF16), u2_ref[...], preferred_element_type=F32)


def _suffix_matrix():
    j = lax.broadcasted_iota(jnp.int32, (LANES, 2 * LANES), 0)
    s = lax.broadcasted_iota(jnp.int32, (LANES, 2 * LANES), 1)
    return jnp.where((s >= LANES) | (j > s), 1.0, 0.0).astype(BF16)


def _attn_prompt_kernel(bias_ref, q_ref, k_ref, v_ref, u2_ref, o_ref, kb_ref, vb_ref, acc_ref, car_ref,
                        qm_ref, spb_ref, ls_ref, wb_ref, *, hd):
    nh, tq = acc_ref.shape[0], acc_ref.shape[1]
    qb = tq // LANES
    seq, lw = q_ref.shape[1], q_ref.shape[2]
    sup = kb_ref.shape[2] // LANES
    tk = sup * LANES
    sup_shift = sup.bit_length() - 1
    hp = pl.program_id(1)
    for s in range(seq // tk):
        cols = slice(s * tk, (s + 1) * tk)
        kb_ref[s] = k_ref[0, 0, :, cols].astype(BF16)
        vb_ref[s] = v_ref[0, 0, :, cols].T.astype(BF16)
    head_of_lane = lax.shift_right_logical(lax.broadcasted_iota(jnp.int32, (tq, lw), 1), hd.bit_length() - 1)
    first_of_pair = lax.broadcasted_iota(jnp.int32, (tq, LANES), 1) < hd
    causal = lax.broadcasted_iota(jnp.int32, (tq, tq), 1) < lax.broadcasted_iota(jnp.int32, (tq, tq), 0)
    scale = hd ** -0.5 * LOG2E
    bias = [bias_ref[nh * hp + h] * LOG2E for h in range(nh)]
    heads = range(nh)
    pair_lanes = lambda h: slice((h // 2) * LANES, (h // 2 + 1) * LANES)

    own = lambda nb, k: causal[:, (k - (nb - qb)) * LANES:(k - (nb - qb) + 1) * LANES]

    def scores_matmul(task):
        i, s, nb, diag = task
        if diag:
            q = q_ref[0, i * tq:(i + 1) * tq, :] * scale
            for h in heads:
                qm_ref[h] = jnp.where(head_of_lane == h, q, 0.0).astype(BF16)
        return [jnp.dot(qm_ref[h], kb_ref[s, :, :nb * LANES], preferred_element_type=F32) + bias[h] for h in heads]

    def scores_finish(task, slot, z):
        i, s, nb, diag = task
        for h in heads:
            sp, ls = _stick_terms(z[h])
            spk = [sp[:, k * LANES:(k + 1) * LANES] for k in range(nb)]
            if diag:
                for k in range(nb - qb, nb):
                    spk[k] = jnp.where(own(nb, k), spk[k], 0.0)
            spb_ref[slot, h, :nb * tq, :] = jnp.concatenate(spk, axis=0).astype(BF16)
            ls_ref[slot, h, :, :nb * LANES] = ls

    def weights_matmul(task, slot):
        i, s, nb, diag = task
        return [jnp.dot(spb_ref[slot, h, :nb * tq, :], u2_ref[...], preferred_element_type=F32) for h in heads]

    def weights_finish(task, slot, c):
        i, s, nb, diag = task
        for h in heads:
            carry = None if diag else car_ref[h]
            wh = [None] * nb
            for k in reversed(range(nb)):
                rows = slice(k * tq, (k + 1) * tq)
                t = ls_ref[slot, h, :, k * LANES:(k + 1) * LANES] - c[h][rows, :LANES]
                if carry is not None:
                    t = t - carry
                e = jnp.exp2(t)
                if diag and k >= nb - qb:
                    e = jnp.where(own(nb, k), e, 0.0)
                wh[k] = e.astype(BF16)
                tot = c[h][rows, LANES:]
                carry = tot if carry is None else carry + tot
            car_ref[h] = carry
            wb_ref[slot, h, :, :nb * LANES] = jnp.concatenate(wh, axis=1)

    def values(task, slot, last):
        i, s, nb, diag = task
        pv = [jnp.dot(wb_ref[slot, h, :, :nb * LANES], vb_ref[s, :nb * LANES, pair_lanes(h)],
                      preferred_element_type=F32) for h in heads]
        acc = [pv[h] if diag else acc_ref[h] + pv[h] for h in heads]
        if last:
            o_ref[0, i * tq:(i + 1) * tq, :] = jnp.concatenate(
                [jnp.where(first_of_pair, acc[h], acc[h + 1]) for h in range(0, nh, 2)], axis=1)
        else:
            for h in heads:
                acc_ref[h] = acc[h]

    tasks = []
    for i in range(seq // tq):
        last_block = i * qb + (qb - 1)
        s_own = last_block >> sup_shift
        tasks.append((i, s_own, (last_block & (sup - 1)) + 1, True))
        tasks.extend((i, s, sup, False) for s in range(s_own - 1, -1, -1))
    closes_step = [n + 1 == len(tasks) or tasks[n + 1][3] for n in range(len(tasks))]

    for n in range(len(tasks) + 2):
        z = scores_matmul(tasks[n]) if n < len(tasks) else None
        c = weights_matmul(tasks[n - 1], (n - 1) % 2) if 1 <= n <= len(tasks) else None
        if n >= 2:
            values(tasks[n - 2], n % 2, closes_step[n - 2])
        if z is not None:
            scores_finish(tasks[n], n % 2, z)
        if c is not None:
            weights_finish(tasks[n - 1], (n - 1) % 2, c)


def _attn_prompt(q, k_t, v_t, layer, bias, u2, hd):
    b, seq, aw = q.shape
    assert 2 * hd == LANES and seq % LANES == 0 and aw % LANES == 0
    sup = max(s for s in (1, 2, ATTN_GROUP) if seq % (s * LANES) == 0 and s <= ATTN_GROUP)
    tk = sup * LANES
    tq = ATTN_Q_ROWS if tk % ATTN_Q_ROWS == 0 else LANES
    lw = ATTN_HEADS * hd if aw % (ATTN_HEADS * hd) == 0 else LANES
    nh = lw // hd
    blk = pl.BlockSpec((1, seq, lw), lambda bi, hp: (bi, 0, hp))
    blk_t = pl.BlockSpec((1, 1, lw, seq), lambda bi, hp: (layer, bi, hp, 0))
    return pl.pallas_call(
        functools.partial(_attn_prompt_kernel, hd=hd),
        grid=(b, aw // lw),
        in_specs=[pl.BlockSpec(memory_space=pltpu.SMEM), blk, blk_t, blk_t, _const_spec((LANES, 2 * LANES))],
        out_specs=blk,
        out_shape=jax.ShapeDtypeStruct((b, seq, aw), F32),
        scratch_shapes=[pltpu.VMEM((seq // tk, lw, tk), BF16), pltpu.VMEM((seq // tk, tk, lw), BF16),
                        pltpu.VMEM((nh, tq, LANES), F32), pltpu.VMEM((nh, tq, LANES), F32),
                        pltpu.VMEM((nh, tq, lw), BF16), pltpu.VMEM((2, nh, sup * tq, LANES), BF16),
                        pltpu.VMEM((2, nh, tq, tk), F32), pltpu.VMEM((2, nh, tq, tk), BF16)],
        compiler_params=_params("parallel", "parallel"),
        name="attn_prompt",
    )(bias, q, k_t, v_t, u2)


def _attn_decode_kernel(pt_ref, q_ref, bias_ref, *refs, hd, npp):
    del pt_ref
    k_refs, v_refs = refs[:npp], refs[npp:2 * npp]
    u2_ref, o_ref, qbd_ref, acc_ref, car_ref = refs[2 * npp:]
    j = pl.program_id(1)
    rows, aw = qbd_ref.shape
    own = (lax.broadcasted_iota(jnp.int32, (rows, aw), 1) // hd) == lax.broadcasted_iota(jnp.int32, (rows, aw), 0)

    @pl.when(j == 0)
    def _():
        q = jnp.broadcast_to(q_ref[0] * (hd ** -0.5 * LOG2E), (rows, aw))
        qbd_ref[...] = jnp.where(own, q, 0.0).astype(BF16)
        acc_ref[...] = jnp.zeros_like(acc_ref)
        car_ref[...] = jnp.zeros_like(car_ref)

    qbd = qbd_ref[...]
    bias2 = bias_ref[...] * LOG2E
    terms = [_stick_terms(jnp.dot(qbd, k_refs[p][0, 0].astype(BF16), preferred_element_type=F32) + bias2)
             for p in range(npp)]
    c = _suffix_sums(jnp.concatenate([t[0] for t in terms], axis=0), u2_ref)
    carry = car_ref[...]
    pv = None
    for p in range(npp):
        r = slice(p * rows, (p + 1) * rows)
        w = jnp.exp2(terms[p][1] - c[r, :LANES] - carry).astype(BF16)
        carry = carry + c[r, LANES:]
        part = lax.dot_general(w, v_refs[p][0, 0].astype(BF16), _NT, preferred_element_type=F32)
        pv = part if pv is None else pv + part
    car_ref[...] = carry
    acc_ref[...] += pv

    @pl.when(j == pl.num_programs(1) - 1)
    def _():
        o_ref[0] = jnp.sum(jnp.where(own, acc_ref[...], 0.0), axis=0, keepdims=True)


def _attn_decode(q, bias, cache_k, cache_v, page_table, layer, u2, hd):
    s, aw = q.shape
    n_pages = page_table.shape[1]
    page = cache_k.shape[3]
    nh = aw // hd
    rows = 2 * SUBLANES
    assert page == LANES and nh <= rows
    bias_rows = jnp.broadcast_to(jnp.pad(bias, (0, rows - nh))[:, None], (rows, LANES))
    npp = max(n for n in range(1, DECODE_PAGES + 1) if n_pages % n == 0)

    def kv_spec(p):
        return pl.BlockSpec((1, 1, aw, page), lambda b, j, pt: (layer, pt[b, n_pages - 1 - (j * npp + p)], 0, 0))

    q_spec = pl.BlockSpec((1, 1, aw), lambda b, j, pt: (b, 0, 0))
    pages = [kv_spec(p) for p in range(npp)]
    out = pl.pallas_call(
        functools.partial(_attn_decode_kernel, hd=hd, npp=npp),
        grid_spec=pltpu.PrefetchScalarGridSpec(
            num_scalar_prefetch=1,
            grid=(s, n_pages // npp),
            in_specs=[q_spec, pl.BlockSpec((rows, LANES), lambda b, j, pt: (0, 0))] + pages + pages +
                     [pl.BlockSpec((LANES, 2 * LANES), lambda b, j, pt: (0, 0))],
            out_specs=q_spec,
            scratch_shapes=[pltpu.VMEM((rows, aw), BF16), pltpu.VMEM((rows, aw), F32),
                            pltpu.VMEM((rows, LANES), F32)],
        ),
        out_shape=jax.ShapeDtypeStruct((s, 1, aw), F32),
        compiler_params=_params("parallel", "arbitrary"),
        name="attn_decode",
    )(page_table, q.reshape(s, 1, aw), bias_rows, *([cache_k] * npp), *([cache_v] * npp), u2)
    return out.reshape(s, aw)


def _ssm_kernel(u_ref, bbd_ref, cbd_ref, ab_ref, d_ref, wglu_ref, h0_ref, y_ref, ht_ref, *scratch,
                rows, steps, cw):
    bu_refs, h_ref = scratch[:-1], scratch[-1]
    parts = len(bu_refs)
    psteps = steps // parts
    prow = psteps * rows
    c = pl.program_id(0)
    sp = h_ref.shape[1] // 2

    @pl.when(c == 0)
    def _():
        h_ref[...] = h0_ref[...]

    sw = u_ref.shape[1]
    halves = [(slice(k * (sw // 2), (k + 1) * (sw // 2)), k * (sp // 2), (k + 1) * (sp // 2)) for k in range(2)]
    for p in range(parts):
        u = u_ref[p * prow:(p + 1) * prow, :]
        for ch, lo, hi in halves:
            for off in (0, sp):
                bu_refs[p][:, off + lo:off + hi] = _mm(u[:, ch], bbd_ref[ch, off + lo:off + hi])

    for p in range(parts):
        bu_ref = bu_refs[p]
        for cc in range(sp // cw):
            re = slice(cc * cw, (cc + 1) * cw)
            im = slice(sp + cc * cw, sp + (cc + 1) * cw)
            a_r = jnp.broadcast_to(ab_ref[0:1, re], (rows, cw))
            a_i = jnp.broadcast_to(ab_ref[0:1, im], (rows, cw))
            h_r, h_i = h_ref[:, re], h_ref[:, im]
            for t in range(psteps):
                r = slice(t * rows, (t + 1) * rows)
                h_r, h_i = a_r * h_r - a_i * h_i + bu_ref[r, re], a_r * h_i + a_i * h_r + bu_ref[r, im]
                bu_ref[r, re] = h_r
                bu_ref[r, im] = h_i
            h_ref[:, re] = h_r
            h_ref[:, im] = h_i

        y = jnp.concatenate(
            [_mm(bu_ref[:, lo:hi], cbd_ref[lo:hi, ch]) + _mm(bu_ref[:, sp + lo:sp + hi], cbd_ref[sp + lo:sp + hi, ch])
             for ch, lo, hi in halves], axis=1) + d_ref[...] * u_ref[p * prow:(p + 1) * prow, :]
        y = jax.nn.gelu(y, approximate=True)
        y_ref[p * prow:(p + 1) * prow, :] = y * jax.nn.sigmoid(_mm(y, wglu_ref[...]))

    @pl.when(c == pl.num_programs(0) - 1)
    def _():
        ht_ref[...] = h_ref[...]


def _ssm(u, bbd, cbd, ab, d_skip, w_glu, h0, rows, steps):
    n, sw = u.shape
    sp2 = bbd.shape[1]
    chunk = rows * steps
    cw = 4 * LANES * SUBLANES // rows if rows <= 4 * SUBLANES else LANES
    cw = max(LANES, min(cw, sp2 // 2))
    parts = SSM_PARTS if steps % SSM_PARTS == 0 else 1
    blk = pl.BlockSpec((chunk, sw), lambda c: (c, 0))
    return pl.pallas_call(
        functools.partial(_ssm_kernel, rows=rows, steps=steps, cw=cw),
        grid=(n // chunk,),
        in_specs=[blk, _const_spec((sw, sp2)), _const_spec((sp2, sw)), _const_spec((SUBLANES, sp2)),
                  _const_spec((1, sw)), _const_spec((sw, sw)), _const_spec((rows, sp2))],
        out_specs=[blk, pl.BlockSpec((rows, sp2), lambda c: (0, 0))],
        out_shape=[jax.ShapeDtypeStruct((n, sw), F32), jax.ShapeDtypeStruct((rows, sp2), F32)],
        scratch_shapes=[pltpu.VMEM((chunk // parts, sp2), F32)] * parts + [pltpu.VMEM((rows, sp2), F32)],
        compiler_params=_params("arbitrary"),
        name="ssm",
    )(u, bbd, cbd, ab, d_skip.reshape(1, sw), w_glu, h0)


def _tail_kernel(h_ref, att_ref, ssm_ref, p_ref, ga_ref, gs_ref, woa_ref, wos_ref, gf_ref, wup_ref, wdn_ref,
                 gp_ref, wg_ref, wp_ref, gfin_ref, o_ref, acc_ref, xn_ref, *, n_sub, final):
    c = pl.program_id(1)

    @pl.when(c == 0)
    def _():
        na = _rms(att_ref[...], ga_ref[...])
        ns = _rms(ssm_ref[...], gs_ref[...])
        h1 = h_ref[...] + _mm(na, woa_ref[...]) + _mm(ns, wos_ref[...])
        acc_ref[...] = h1
        xn_ref[...] = _rms(h1, gf_ref[...]).astype(xn_ref.dtype)

    xn = xn_ref[...]
    sub = wup_ref.shape[1] // n_sub
    ffn = None
    for s in range(n_sub):
        f = _mm(xn, wup_ref[:, s * sub:(s + 1) * sub])
        part = _mm(jnp.square(jnp.maximum(f, 0.0)), wdn_ref[s * sub:(s + 1) * sub, :])
        ffn = part if ffn is None else ffn + part
    acc_ref[...] += ffn

    @pl.when(c == pl.num_programs(1) - 1)
    def _():
        h2 = acc_ref[...]
        gate = jax.nn.sigmoid(_mm(_rms(h2, gp_ref[...]), wg_ref[...]))
        h3 = h2 + _mm(p_ref[...], wp_ref[...]) * gate
        o_ref[...] = _rms(h3, gfin_ref[...]) if final else h3


def _tail(h, att, ssm, p, lw, g_final, tm, ff_chunks, n_sub, final, time_major_batch=None):
    t, d = h.shape
    aw, sw, pd = att.shape[1], lw["w_out_s"].shape[0], p.shape[1]
    d_ff = lw["w_up"].shape[1]
    ffc = d_ff // ff_chunks
    nt = t // tm
    wdt = lw["w_up"].dtype
    if time_major_batch is None:
        ssm_spec = pl.BlockSpec((tm, sw), lambda i, c: (i, 0))
    else:
        nlt = nt // time_major_batch
        ssm_spec = pl.BlockSpec((tm, sw), lambda i, c: (i % nlt, i // nlt))
    tok = lambda w: pl.BlockSpec((tm, w), lambda i, c: (i, 0))
    if ff_chunks == 1:
        up_spec, dn_spec = _const_spec((d, d_ff)), _const_spec((d_ff, d))
    else:
        up_spec = pl.BlockSpec((d, ffc), lambda i, c: (0, c))
        dn_spec = pl.BlockSpec((ffc, d), lambda i, c: (c, 0))
    vec = lambda g: g.reshape(1, -1)
    return pl.pallas_call(
        functools.partial(_tail_kernel, n_sub=n_sub, final=final),
        grid=(nt, ff_chunks),
        in_specs=[tok(d), tok(aw), ssm_spec, tok(pd),
                  _const_spec((1, aw)), _const_spec((1, sw)), _const_spec((aw, d)), _const_spec((sw, d)),
                  _const_spec((1, d)), up_spec, dn_spec,
                  _const_spec((1, d)), _const_spec((d, d)), _const_spec((pd, d)), _const_spec((1, d))],
        out_specs=tok(d),
        out_shape=jax.ShapeDtypeStruct((t, d), F32),
        scratch_shapes=[pltpu.VMEM((tm, d), F32), pltpu.VMEM((tm, d), wdt)],
        compiler_params=_params("parallel", "arbitrary"),
        name="tail",
    )(h, att, ssm, p, vec(lw["g_att_out"]), vec(lw["g_ssm_out"]), lw["w_out_a"], lw["w_out_s"],
      vec(lw["g_ffn"]), lw["w_up"], lw["w_down"], vec(lw["g_ple"]), lw["w_ple_gate"], lw["w_ple"], vec(g_final))


PROMPT_TILE = 512
SSM_STEPS = 64
SSM_PARTS = 2
FF_SUB = 1024


def kernel(x_prompt, x_sample, p_prompt, p_sample, cache_k, cache_v, state_ssm_re, state_ssm_im, page_table, w_in, w_out, g_mix, g_att_out, g_ssm_out, att_bias, ssm_a_re, ssm_a_im, ssm_log_dt, ssm_b_re, ssm_b_im, ssm_c_re, ssm_c_im, ssm_d, w_glu, g_ffn, w_up, w_down, w_ple, w_ple_gate, g_ple, g_final):
    b, seq, d = x_prompt.shape
    sb, st = x_sample.shape[0], x_sample.shape[1]
    assert st == 1 and b % SUBLANES == 0
    depth, n_groups, n_state = ssm_a_re.shape
    n_heads = att_bias.shape[1]
    hd = cache_k.shape[-1]
    aw = n_heads * hd
    sw = n_groups * ssm_b_re.shape[-1]
    sp = n_groups * n_state
    d_ff = w_up.shape[2]
    tm = min(PROMPT_TILE, seq)
    steps = min(SSM_STEPS, seq)
    n_sub = max(1, d_ff // FF_SUB)

    bbd, cbd, ab = _ssm_prep(ssm_a_re, ssm_a_im, ssm_log_dt, ssm_b_re, ssm_b_im, ssm_c_re, ssm_c_im)
    u2 = _suffix_matrix()
    pages_t = lambda c: c.transpose(0, 1, 3, 4, 2).reshape(depth, c.shape[1], aw, c.shape[2])
    cache_k, cache_v = pages_t(cache_k), pages_t(cache_v)

    def layer_weights(l, dt, kv_transposed):
        cast = lambda w: w[l].astype(dt)
        kv = (lambda w: cast(w).T) if kv_transposed else cast
        return {"w_q": cast(w_in[:, :, :aw]), "w_k": kv(w_in[:, :, aw:2 * aw]), "w_v": kv(w_in[:, :, 2 * aw:3 * aw]),
                "w_u": cast(w_in[:, :, 3 * aw:]), "w_out_a": cast(w_out[:, :aw]), "w_out_s": cast(w_out[:, aw:]),
                "w_up": cast(w_up), "w_down": cast(w_down), "w_ple": cast(w_ple),
                "w_ple_gate": cast(w_ple_gate), "w_glu": cast(w_glu), "bbd": cast(bbd), "cbd": cast(cbd),
                "g_att_out": g_att_out[l], "g_ssm_out": g_ssm_out[l], "g_ffn": g_ffn[l], "g_ple": g_ple[l]}

    hp = x_prompt.reshape(b * seq, d)
    hs = x_sample.reshape(sb, d)
    zero_state = jnp.zeros((b, 2 * sp), F32)
    outs = {name: [] for name in ("rp", "ip", "ks", "vs", "rs", "is")}
    kv_all = (jnp.zeros((depth, b, aw, seq), F32), jnp.zeros((depth, b, aw, seq), F32))
    for l in range(depth):
        final = l == depth - 1
        lw = layer_weights(l, BF16, True)
        q, k_all, v_all, u = _in_proj(hp, g_mix[l], lw, tm, prompt_batch=b, layer=l, depth=depth, kv_all=kv_all)
        kv_all = (k_all, v_all)
        att = _attn_prompt(q.reshape(b, seq, aw), k_all, v_all, l, att_bias[l], u2, hd)
        y, h_last = _ssm(u.reshape(seq * b, sw), lw["bbd"], lw["cbd"], ab[l], ssm_d[l].reshape(-1),
                         lw["w_glu"], zero_state, rows=b, steps=steps)
        hp = _tail(hp, att.reshape(b * seq, aw), y.reshape(seq, b * sw), p_prompt[l].reshape(b * seq, -1),
                   lw, g_final, tm, 1, n_sub, final, time_major_batch=b)
        outs["rp"].append(h_last[:, :sp].reshape(b, n_groups, n_state))
        outs["ip"].append(h_last[:, sp:].reshape(b, n_groups, n_state))

        lw = layer_weights(l, BF16, False)
        q, k, v, u = _in_proj(hs, g_mix[l], lw, sb)
        att = _attn_decode(q, att_bias[l], cache_k, cache_v, page_table, l, u2, hd)
        h0 = jnp.concatenate([state_ssm_re[l].reshape(sb, sp), state_ssm_im[l].reshape(sb, sp)], axis=1)
        y, h_last = _ssm(u, lw["bbd"], lw["cbd"], ab[l], ssm_d[l].reshape(-1), lw["w_glu"], h0,
                         rows=sb, steps=1)
        hs = _tail(hs, att, y, p_sample[l].reshape(sb, -1), lw, g_final, sb, 1, n_sub, final)
        outs["ks"].append(k.reshape(sb, 1, n_heads, hd))
        outs["vs"].append(v.reshape(sb, 1, n_heads, hd))
        outs["rs"].append(h_last[:, :sp].reshape(sb, n_groups, n_state))
        outs["is"].append(h_last[:, sp:].reshape(sb, n_groups, n_state))

    stack = lambda name: jnp.stack(outs[name])
    seq_major = lambda x: x.reshape(depth, b, n_heads, hd, seq).transpose(0, 1, 4, 2, 3)
    return (hp.reshape(b, seq, d), hs.reshape(sb, st, d),
            seq_major(kv_all[0]), seq_major(kv_all[1]), stack("rp"), stack("ip"),
            stack("ks"), stack("vs"), stack("rs"), stack("is"))
```

```python
import functools

import jax
import jax.numpy as jnp
from jax import lax
from jax.experimental import pallas as pl
from jax.experimental.pallas import tpu as pltpu

RMS_EPS = 1e-6
LANES = 128
SUBLANES = 8
VMEM_LIMIT = 56 * 1024 * 1024
BF16 = jnp.bfloat16
F32 = jnp.float32
_NT = (((1,), (1,)), ((), ()))
LOG2E = 1.4426950408889634
ATTN_GROUP = 4
ATTN_Q_ROWS = 256
ATTN_HEADS = 2
DECODE_PAGES = 16


def _mm(a, w):
    if w.dtype == BF16:
        return jnp.dot(a.astype(BF16), w, preferred_element_type=F32)
    return jnp.dot(a, w, preferred_element_type=F32, precision=lax.Precision.HIGHEST)


def _rms(x, g):
    return x * lax.rsqrt(jnp.mean(x * x, axis=-1, keepdims=True) + RMS_EPS) * g


def _const_spec(shape):
    nd = len(shape)
    return pl.BlockSpec(shape, lambda *_: (0,) * nd, pipeline_mode=pl.Buffered(1))


def _params(*sem):
    return pltpu.CompilerParams(dimension_semantics=sem, vmem_limit_bytes=VMEM_LIMIT)


def _ssm_prep_kernel(are_ref, aim_ref, ldt_ref, bre_ref, bim_ref, cre_ref, cim_ref,
                     bbd_ref, cbd_ref, ab_ref, *, c_shift, p_shift):
    a_re = are_ref[0]
    a_im = aim_ref[0]
    dt = jnp.exp(ldt_ref[0])
    mag = jnp.exp(a_re * dt)
    ab_re = mag * jnp.cos(a_im * dt)
    ab_im = mag * jnp.sin(a_im * dt)
    den = a_re * a_re + a_im * a_im
    f_re = ((ab_re - 1.0) * a_re + ab_im * a_im) / den
    f_im = (ab_im * a_re - (ab_re - 1.0) * a_im) / den
    cb, sp = bre_ref.shape[1], bre_ref.shape[2]
    ch0 = pl.program_id(1) * cb
    same_bc = ((lax.broadcasted_iota(jnp.int32, (cb, sp), 0) + ch0) >> c_shift) == \
              (lax.broadcasted_iota(jnp.int32, (cb, sp), 1) >> p_shift)
    br = bre_ref[0]
    bi = bim_ref[0]
    bbd_ref[0, :, :sp] = jnp.where(same_bc, f_re * br - f_im * bi, 0.0)
    bbd_ref[0, :, sp:] = jnp.where(same_bc, f_re * bi + f_im * br, 0.0)
    same_cb = (lax.broadcasted_iota(jnp.int32, (sp, cb), 0) >> p_shift) == \
              ((lax.broadcasted_iota(jnp.int32, (sp, cb), 1) + ch0) >> c_shift)
    cbd_ref[0, :sp, :] = jnp.where(same_cb, cre_ref[0], 0.0)
    cbd_ref[0, sp:, :] = jnp.where(same_cb, -cim_ref[0], 0.0)
    ab_ref[0, :, :sp] = jnp.broadcast_to(ab_re, (SUBLANES, sp))
    ab_ref[0, :, sp:] = jnp.broadcast_to(ab_im, (SUBLANES, sp))


def _ssm_prep(a_re, a_im, log_dt, b_re, b_im, c_re, c_im):
    depth, g, p = a_re.shape
    c = b_re.shape[-1]
    sw, sp = g * c, g * p
    assert c & (c - 1) == 0 and p & (p - 1) == 0 and sw % LANES == 0
    flat = lambda x: x.reshape(depth, 1, sp)
    ldt = jnp.repeat(log_dt, p, axis=1).reshape(depth, 1, sp)
    tile_b = lambda b: jnp.tile(b.transpose(0, 1, 3, 2).reshape(depth, sw, p), (1, 1, g))
    tile_c = lambda x: jnp.tile(x.transpose(0, 3, 1, 2).reshape(depth, p, sw), (1, g, 1))
    vec = pl.BlockSpec((1, 1, sp), lambda l, j: (l, 0, 0))
    bspec = pl.BlockSpec((1, LANES, sp), lambda l, j: (l, j, 0))
    cspec = pl.BlockSpec((1, sp, LANES), lambda l, j: (l, 0, j))
    return pl.pallas_call(
        functools.partial(_ssm_prep_kernel, c_shift=c.bit_length() - 1, p_shift=p.bit_length() - 1),
        grid=(depth, sw // LANES),
        in_specs=[vec, vec, vec, bspec, bspec, cspec, cspec],
        out_specs=[pl.BlockSpec((1, LANES, 2 * sp), lambda l, j: (l, j, 0)),
                   pl.BlockSpec((1, 2 * sp, LANES), lambda l, j: (l, 0, j)),
                   pl.BlockSpec((1, SUBLANES, 2 * sp), lambda l, j: (l, 0, 0))],
        out_shape=[jax.ShapeDtypeStruct((depth, sw, 2 * sp), F32),
                   jax.ShapeDtypeStruct((depth, 2 * sp, sw), F32),
                   jax.ShapeDtypeStruct((depth, SUBLANES, 2 * sp), F32)],
        compiler_params=_params("arbitrary", "arbitrary"),
        name="ssm_prep",
    )(flat(a_re), flat(a_im), ldt, tile_b(b_re), tile_b(b_im), tile_c(c_re), tile_c(c_im))


def _in_proj_kernel(x_ref, g_ref, wq_ref, wk_ref, wv_ref, wu_ref, *rest, kv_transposed):
    q_ref, k_ref, v_ref, u_ref = rest[-4:]
    xn = _rms(x_ref[...], g_ref[...])
    if wq_ref.dtype == BF16:
        xn = xn.astype(BF16)
    q_ref[...] = _mm(xn, wq_ref[...])
    u_ref[...] = _mm(xn, wu_ref[...])
    if kv_transposed:
        k_ref[0, 0] = lax.dot_general(wk_ref[...], xn, _NT, preferred_element_type=F32)
        v_ref[0, 0] = lax.dot_general(wv_ref[...], xn, _NT, preferred_element_type=F32)
    else:
        k_ref[...] = _mm(xn, wk_ref[...])
        v_ref[...] = _mm(xn, wv_ref[...])


def _in_proj(x, g, lw, tm, prompt_batch=None, layer=None, depth=None, kv_all=None):
    t, d = x.shape
    aw, sw = lw["w_q"].shape[1], lw["w_u"].shape[1]
    nt = t // tm
    row = pl.BlockSpec((tm, aw), lambda i: (i, 0))
    extra_in, extra_specs, aliases = [], [], {}
    if prompt_batch is None:
        kv_spec, kv_shape = row, (t, aw)
        u_spec, u_shape = pl.BlockSpec((tm, sw), lambda i: (i, 0)), (t, sw)
    else:
        nlt = nt // prompt_batch
        kv_spec = pl.BlockSpec((1, 1, aw, tm), lambda i: (layer, i // nlt, 0, i % nlt))
        kv_shape = (depth, prompt_batch, aw, t // prompt_batch)
        u_spec = pl.BlockSpec((tm, sw), lambda i: (i % nlt, i // nlt))
        u_shape = (t // prompt_batch, prompt_batch * sw)
        extra_in = list(kv_all)
        extra_specs = [pl.BlockSpec(memory_space=pl.ANY)] * 2
        aliases = {6: 1, 7: 2}
    return pl.pallas_call(
        functools.partial(_in_proj_kernel, kv_transposed=prompt_batch is not None),
        grid=(nt,),
        in_specs=[pl.BlockSpec((tm, d), lambda i: (i, 0)), _const_spec((1, d)), _const_spec(lw["w_q"].shape),
                  _const_spec(lw["w_k"].shape), _const_spec(lw["w_v"].shape), _const_spec(lw["w_u"].shape)]
                 + extra_specs,
        out_specs=[row, kv_spec, kv_spec, u_spec],
        out_shape=[jax.ShapeDtypeStruct((t, aw), F32), jax.ShapeDtypeStruct(kv_shape, F32),
                   jax.ShapeDtypeStruct(kv_shape, F32), jax.ShapeDtypeStruct(u_shape, F32)],
        input_output_aliases=aliases,
        compiler_params=_params("parallel"),
        name="in_proj",
    )(x, g.reshape(1, d), lw["w_q"], lw["w_k"], lw["w_v"], lw["w_u"], *extra_in)


def _stick_terms(z2):
    sp = jnp.maximum(z2, 0.0) + jnp.log2(1.0 + jnp.exp2(-jnp.abs(z2)))
    return sp, z2 - sp


def _suffix_sums(sp, u2_ref):
    return jnp.dot(sp.astype(BF16), u2_ref[...], preferred_element_type=F32)


def _suffix_matrix():
    j = lax.broadcasted_iota(jnp.int32, (LANES, 2 * LANES), 0)
    s = lax.broadcasted_iota(jnp.int32, (LANES, 2 * LANES), 1)
    return jnp.where((s >= LANES) | (j > s), 1.0, 0.0).astype(BF16)


def _attn_prompt_kernel(bias_ref, q_ref, k_ref, v_ref, u2_ref, o_ref, kb_ref, vb_ref, acc_ref, car_ref,
                        qm_ref, spb_ref, ls_ref, wb_ref, *, hd):
    nh, tq = acc_ref.shape[0], acc_ref.shape[1]
    qb = tq // LANES
    seq, lw = q_ref.shape[1], q_ref.shape[2]
    sup = kb_ref.shape[2] // LANES
    tk = sup * LANES
    sup_shift = sup.bit_length() - 1
    hp = pl.program_id(1)
    ones_rows = jnp.where(lax.broadcasted_iota(jnp.int32, (lw, tk), 0) < 2, 1.0, 0.0).astype(BF16)
    for s in range(seq // tk):
        cols = slice(s * tk, (s + 1) * tk)
        kb_ref[s] = jnp.concatenate([k_ref[0, 0, :, cols].astype(BF16), ones_rows], axis=0)
        vb_ref[s] = v_ref[0, 0, :, cols].T.astype(BF16)
    lane_q = lax.broadcasted_iota(jnp.int32, (tq, lw), 1)
    head_of_lane = lax.shift_right_logical(lane_q, hd.bit_length() - 1)
    first_of_pair = lax.broadcasted_iota(jnp.int32, (tq, LANES), 1) < hd
    causal = lax.broadcasted_iota(jnp.int32, (tq, tq), 1) < lax.broadcasted_iota(jnp.int32, (tq, tq), 0)
    scale = hd ** -0.5 * LOG2E
    heads = range(nh)
    bias_lanes = []
    for h in heads:
        b2 = jnp.full((tq, lw), bias_ref[nh * hp + h] * LOG2E, F32)
        b_hi = b2.astype(BF16).astype(F32)
        bias_lanes.append(jnp.where(lane_q == 0, b_hi, jnp.where(lane_q == 1, b2 - b_hi, 0.0)).astype(BF16))
    pair_lanes = lambda h: slice((h // 2) * LANES, (h // 2 + 1) * LANES)

    own = lambda nb, k: causal[:, (k - (nb - qb)) * LANES:(k - (nb - qb) + 1) * LANES]

    def scores_matmul(task):
        i, s, nb, diag = task
        if diag:
            q = q_ref[0, i * tq:(i + 1) * tq, :] * scale
            for h in heads:
                qm_ref[h] = jnp.concatenate([jnp.where(head_of_lane == h, q, 0.0).astype(BF16), bias_lanes[h]], axis=1)
        return [jnp.dot(qm_ref[h], kb_ref[s, :, :nb * LANES], preferred_element_type=F32) for h in heads]

    def scores_finish(task, slot, z):
        i, s, nb, diag = task
        for h in heads:
            sp, ls = _stick_terms(z[h])
            spk = [sp[:, k * LANES:(k + 1) * LANES] for k in range(nb)]
            if diag:
                for k in range(nb - qb, nb):
                    spk[k] = jnp.where(own(nb, k), spk[k], 0.0)
            spb_ref[slot, h, :nb * tq, :] = jnp.concatenate(spk, axis=0).astype(BF16)
            ls_ref[slot, h, :, :nb * LANES] = ls

    def weights_matmul(task, slot):
        i, s, nb, diag = task
        return [jnp.dot(spb_ref[slot, h, :nb * tq, :], u2_ref[...], preferred_element_type=F32) for h in heads]

    def weights_finish(task, slot, c):
        i, s, nb, diag = task
        for h in heads:
            carry = None if diag else car_ref[h]
            wh = [None] * nb
            for k in reversed(range(nb)):
                rows = slice(k * tq, (k + 1) * tq)
                t = ls_ref[slot, h, :, k * LANES:(k + 1) * LANES] - c[h][rows, :LANES]
                if carry is not None:
                    t = t - carry
                e = jnp.exp2(t)
                if diag and k >= nb - qb:
                    e = jnp.where(own(nb, k), e, 0.0)
                wh[k] = e.astype(BF16)
                tot = c[h][rows, LANES:]
                carry = tot if carry is None else carry + tot
            car_ref[h] = carry
            wb_ref[slot, h, :, :nb * LANES] = jnp.concatenate(wh, axis=1)

    def values(task, slot, last):
        i, s, nb, diag = task
        pv = [jnp.dot(wb_ref[slot, h, :, :nb * LANES], vb_ref[s, :nb * LANES, pair_lanes(h)],
                      preferred_element_type=F32) for h in heads]
        acc = [pv[h] if diag else acc_ref[h] + pv[h] for h in heads]
        if last:
            o_ref[0, i * tq:(i + 1) * tq, :] = jnp.concatenate(
                [jnp.where(first_of_pair, acc[h], acc[h + 1]) for h in range(0, nh, 2)], axis=1)
        else:
            for h in heads:
                acc_ref[h] = acc[h]

    tasks = []
    for i in range(seq // tq):
        last_block = i * qb + (qb - 1)
        s_own = last_block >> sup_shift
        tasks.append((i, s_own, (last_block & (sup - 1)) + 1, True))
        tasks.extend((i, s, sup, False) for s in range(s_own - 1, -1, -1))
    closes_step = [n + 1 == len(tasks) or tasks[n + 1][3] for n in range(len(tasks))]

    for n in range(len(tasks) + 2):
        z = scores_matmul(tasks[n]) if n < len(tasks) else None
        c = weights_matmul(tasks[n - 1], (n - 1) % 2) if 1 <= n <= len(tasks) else None
        if n >= 2:
            values(tasks[n - 2], n % 2, closes_step[n - 2])
        if z is not None:
            scores_finish(tasks[n], n % 2, z)
        if c is not None:
            weights_finish(tasks[n - 1], (n - 1) % 2, c)


def _attn_prompt(q, k_t, v_t, layer, bias, u2, hd):
    b, seq, aw = q.shape
    assert 2 * hd == LANES and seq % LANES == 0 and aw % LANES == 0
    sup = max(s for s in (1, 2, ATTN_GROUP) if seq % (s * LANES) == 0 and s <= ATTN_GROUP)
    tk = sup * LANES
    tq = ATTN_Q_ROWS if tk % ATTN_Q_ROWS == 0 else LANES
    lw = ATTN_HEADS * hd if aw % (ATTN_HEADS * hd) == 0 else LANES
    nh = lw // hd
    blk = pl.BlockSpec((1, seq, lw), lambda bi, hp: (bi, 0, hp))
    blk_t = pl.BlockSpec((1, 1, lw, seq), lambda bi, hp: (layer, bi, hp, 0))
    return pl.pallas_call(
        functools.partial(_attn_prompt_kernel, hd=hd),
        grid=(b, aw // lw),
        in_specs=[pl.BlockSpec(memory_space=pltpu.SMEM), blk, blk_t, blk_t, _const_spec((LANES, 2 * LANES))],
        out_specs=blk,
        out_shape=jax.ShapeDtypeStruct((b, seq, aw), F32),
        scratch_shapes=[pltpu.VMEM((seq // tk, 2 * lw, tk), BF16), pltpu.VMEM((seq // tk, tk, lw), BF16),
                        pltpu.VMEM((nh, tq, LANES), F32), pltpu.VMEM((nh, tq, LANES), F32),
                        pltpu.VMEM((nh, tq, 2 * lw), BF16), pltpu.VMEM((2, nh, sup * tq, LANES), BF16),
                        pltpu.VMEM((2, nh, tq, tk), F32), pltpu.VMEM((2, nh, tq, tk), BF16)],
        compiler_params=_params("parallel", "parallel"),
        name="attn_prompt",
    )(bias, q, k_t, v_t, u2)


def _attn_decode_kernel(pt_ref, q_ref, bias_ref, *refs, hd, npp):
    del pt_ref
    k_refs, v_refs = refs[:npp], refs[npp:2 * npp]
    u2_ref, o_ref, qbd_ref, acc_ref, car_ref = refs[2 * npp:]
    j = pl.program_id(1)
    rows, aw = qbd_ref.shape
    own = (lax.broadcasted_iota(jnp.int32, (rows, aw), 1) // hd) == lax.broadcasted_iota(jnp.int32, (rows, aw), 0)

    @pl.when(j == 0)
    def _():
        q = jnp.broadcast_to(q_ref[0] * (hd ** -0.5 * LOG2E), (rows, aw))
        qbd_ref[...] = jnp.where(own, q, 0.0).astype(BF16)
        acc_ref[...] = jnp.zeros_like(acc_ref)
        car_ref[...] = jnp.zeros_like(car_ref)

    qbd = qbd_ref[...]
    bias2 = bias_ref[...] * LOG2E
    terms = [_stick_terms(jnp.dot(qbd, k_refs[p][0, 0].astype(BF16), preferred_element_type=F32) + bias2)
             for p in range(npp)]
    c = _suffix_sums(jnp.concatenate([t[0] for t in terms], axis=0), u2_ref)
    carry = car_ref[...]
    pv = None
    for p in range(npp):
        r = slice(p * rows, (p + 1) * rows)
        w = jnp.exp2(terms[p][1] - c[r, :LANES] - carry).astype(BF16)
        carry = carry + c[r, LANES:]
        part = lax.dot_general(w, v_refs[p][0, 0].astype(BF16), _NT, preferred_element_type=F32)
        pv = part if pv is None else pv + part
    car_ref[...] = carry
    acc_ref[...] += pv

    @pl.when(j == pl.num_programs(1) - 1)
    def _():
        o_ref[0] = jnp.sum(jnp.where(own, acc_ref[...], 0.0), axis=0, keepdims=True)


def _attn_decode(q, bias, cache_k, cache_v, page_table, layer, u2, hd):
    s, aw = q.shape
    n_pages = page_table.shape[1]
    page = cache_k.shape[3]
    nh = aw // hd
    rows = 2 * SUBLANES
    assert page == LANES and nh <= rows
    bias_rows = jnp.broadcast_to(jnp.pad(bias, (0, rows - nh))[:, None], (rows, LANES))
    npp = max(n for n in range(1, DECODE_PAGES + 1) if n_pages % n == 0)

    def kv_spec(p):
        return pl.BlockSpec((1, 1, aw, page), lambda b, j, pt: (layer, pt[b, n_pages - 1 - (j * npp + p)], 0, 0))

    q_spec = pl.BlockSpec((1, 1, aw), lambda b, j, pt: (b, 0, 0))
    pages = [kv_spec(p) for p in range(npp)]
    out = pl.pallas_call(
        functools.partial(_attn_decode_kernel, hd=hd, npp=npp),
        grid_spec=pltpu.PrefetchScalarGridSpec(
            num_scalar_prefetch=1,
            grid=(s, n_pages // npp),
            in_specs=[q_spec, pl.BlockSpec((rows, LANES), lambda b, j, pt: (0, 0))] + pages + pages +
                     [pl.BlockSpec((LANES, 2 * LANES), lambda b, j, pt: (0, 0))],
            out_specs=q_spec,
            scratch_shapes=[pltpu.VMEM((rows, aw), BF16), pltpu.VMEM((rows, aw), F32),
                            pltpu.VMEM((rows, LANES), F32)],
        ),
        out_shape=jax.ShapeDtypeStruct((s, 1, aw), F32),
        compiler_params=_params("parallel", "arbitrary"),
        name="attn_decode",
    )(page_table, q.reshape(s, 1, aw), bias_rows, *([cache_k] * npp), *([cache_v] * npp), u2)
    return out.reshape(s, aw)


def _ssm_kernel(u_ref, bbd_ref, cbd_ref, ab_ref, d_ref, wglu_ref, h0_ref, y_ref, ht_ref, *scratch,
                rows, steps, cw):
    bu_refs, h_ref = scratch[:-1], scratch[-1]
    parts = len(bu_refs)
    psteps = steps // parts
    prow = psteps * rows
    c = pl.program_id(0)
    sp = h_ref.shape[1] // 2

    @pl.when(c == 0)
    def _():
        h_ref[...] = h0_ref[...]

    sw = u_ref.shape[1]
    halves = [(slice(k * (sw // 2), (k + 1) * (sw // 2)), k * (sp // 2), (k + 1) * (sp // 2)) for k in range(2)]
    for p in range(parts):
        u = u_ref[p * prow:(p + 1) * prow, :]
        for ch, lo, hi in halves:
            for off in (0, sp):
                bu_refs[p][:, off + lo:off + hi] = _mm(u[:, ch], bbd_ref[ch, off + lo:off + hi])

    for p in range(parts):
        bu_ref = bu_refs[p]
        for cc in range(sp // cw):
            re = slice(cc * cw, (cc + 1) * cw)
            im = slice(sp + cc * cw, sp + (cc + 1) * cw)
            a_r = jnp.broadcast_to(ab_ref[0:1, re], (rows, cw))
            a_i = jnp.broadcast_to(ab_ref[0:1, im], (rows, cw))
            h_r, h_i = h_ref[:, re], h_ref[:, im]
            for t in range(psteps):
                r = slice(t * rows, (t + 1) * rows)
                h_r, h_i = a_r * h_r - a_i * h_i + bu_ref[r, re], a_r * h_i + a_i * h_r + bu_ref[r, im]
                bu_ref[r, re] = h_r
                bu_ref[r, im] = h_i
            h_ref[:, re] = h_r
            h_ref[:, im] = h_i

        y = jnp.concatenate(
            [_mm(bu_ref[:, lo:hi], cbd_ref[lo:hi, ch]) + _mm(bu_ref[:, sp + lo:sp + hi], cbd_ref[sp + lo:sp + hi, ch])
             for ch, lo, hi in halves], axis=1) + d_ref[...] * u_ref[p * prow:(p + 1) * prow, :]
        y = jax.nn.gelu(y, approximate=True)
        y_ref[p * prow:(p + 1) * prow, :] = y * jax.nn.sigmoid(_mm(y, wglu_ref[...]))

    @pl.when(c == pl.num_programs(0) - 1)
    def _():
        ht_ref[...] = h_ref[...]


def _ssm(u, bbd, cbd, ab, d_skip, w_glu, h0, rows, steps):
    n, sw = u.shape
    sp2 = bbd.shape[1]
    chunk = rows * steps
    cw = 4 * LANES * SUBLANES // rows if rows <= 4 * SUBLANES else LANES
    cw = max(LANES, min(cw, sp2 // 2))
    parts = SSM_PARTS if steps % SSM_PARTS == 0 else 1
    blk = pl.BlockSpec((chunk, sw), lambda c: (c, 0))
    return pl.pallas_call(
        functools.partial(_ssm_kernel, rows=rows, steps=steps, cw=cw),
        grid=(n // chunk,),
        in_specs=[blk, _const_spec((sw, sp2)), _const_spec((sp2, sw)), _const_spec((SUBLANES, sp2)),
                  _const_spec((1, sw)), _const_spec((sw, sw)), _const_spec((rows, sp2))],
        out_specs=[blk, pl.BlockSpec((rows, sp2), lambda c: (0, 0))],
        out_shape=[jax.ShapeDtypeStruct((n, sw), F32), jax.ShapeDtypeStruct((rows, sp2), F32)],
        scratch_shapes=[pltpu.VMEM((chunk // parts, sp2), F32)] * parts + [pltpu.VMEM((rows, sp2), F32)],
        compiler_params=_params("arbitrary"),
        name="ssm",
    )(u, bbd, cbd, ab, d_skip.reshape(1, sw), w_glu, h0)


def _tail_kernel(h_ref, att_ref, ssm_ref, p_ref, ga_ref, gs_ref, woa_ref, wos_ref, gf_ref, wup_ref, wdn_ref,
                 gp_ref, wg_ref, wp_ref, gfin_ref, o_ref, acc_ref, xn_ref, *, n_sub, final):
    c = pl.program_id(1)

    @pl.when(c == 0)
    def _():
        na = _rms(att_ref[...], ga_ref[...])
        ns = _rms(ssm_ref[...], gs_ref[...])
        h1 = h_ref[...] + _mm(na, woa_ref[...]) + _mm(ns, wos_ref[...])
        acc_ref[...] = h1
        xn_ref[...] = _rms(h1, gf_ref[...]).astype(xn_ref.dtype)

    xn = xn_ref[...]
    sub = wup_ref.shape[1] // n_sub
    ffn = None
    for s in range(n_sub):
        f = _mm(xn, wup_ref[:, s * sub:(s + 1) * sub])
        part = _mm(jnp.square(jnp.maximum(f, 0.0)), wdn_ref[s * sub:(s + 1) * sub, :])
        ffn = part if ffn is None else ffn + part
    acc_ref[...] += ffn

    @pl.when(c == pl.num_programs(1) - 1)
    def _():
        h2 = acc_ref[...]
        gate = jax.nn.sigmoid(_mm(_rms(h2, gp_ref[...]), wg_ref[...]))
        h3 = h2 + _mm(p_ref[...], wp_ref[...]) * gate
        o_ref[...] = _rms(h3, gfin_ref[...]) if final else h3


def _tail(h, att, ssm, p, lw, g_final, tm, ff_chunks, n_sub, final, time_major_batch=None):
    t, d = h.shape
    aw, sw, pd = att.shape[1], lw["w_out_s"].shape[0], p.shape[1]
    d_ff = lw["w_up"].shape[1]
    ffc = d_ff // ff_chunks
    nt = t // tm
    wdt = lw["w_up"].dtype
    if time_major_batch is None:
        ssm_spec = pl.BlockSpec((tm, sw), lambda i, c: (i, 0))
    else:
        nlt = nt // time_major_batch
        ssm_spec = pl.BlockSpec((tm, sw), lambda i, c: (i % nlt, i // nlt))
    tok = lambda w: pl.BlockSpec((tm, w), lambda i, c: (i, 0))
    if ff_chunks == 1:
        up_spec, dn_spec = _const_spec((d, d_ff)), _const_spec((d_ff, d))
    else:
        up_spec = pl.BlockSpec((d, ffc), lambda i, c: (0, c))
        dn_spec = pl.BlockSpec((ffc, d), lambda i, c: (c, 0))
    vec = lambda g: g.reshape(1, -1)
    return pl.pallas_call(
        functools.partial(_tail_kernel, n_sub=n_sub, final=final),
        grid=(nt, ff_chunks),
        in_specs=[tok(d), tok(aw), ssm_spec, tok(pd),
                  _const_spec((1, aw)), _const_spec((1, sw)), _const_spec((aw, d)), _const_spec((sw, d)),
                  _const_spec((1, d)), up_spec, dn_spec,
                  _const_spec((1, d)), _const_spec((d, d)), _const_spec((pd, d)), _const_spec((1, d))],
        out_specs=tok(d),
        out_shape=jax.ShapeDtypeStruct((t, d), F32),
        scratch_shapes=[pltpu.VMEM((tm, d), F32), pltpu.VMEM((tm, d), wdt)],
        compiler_params=_params("parallel", "arbitrary"),
        name="tail",
    )(h, att, ssm, p, vec(lw["g_att_out"]), vec(lw["g_ssm_out"]), lw["w_out_a"], lw["w_out_s"],
      vec(lw["g_ffn"]), lw["w_up"], lw["w_down"], vec(lw["g_ple"]), lw["w_ple_gate"], lw["w_ple"], vec(g_final))


PROMPT_TILE = 512
SSM_STEPS = 64
SSM_PARTS = 2
FF_SUB = 1024


def kernel(x_prompt, x_sample, p_prompt, p_sample, cache_k, cache_v, state_ssm_re, state_ssm_im, page_table, w_in, w_out, g_mix, g_att_out, g_ssm_out, att_bias, ssm_a_re, ssm_a_im, ssm_log_dt, ssm_b_re, ssm_b_im, ssm_c_re, ssm_c_im, ssm_d, w_glu, g_ffn, w_up, w_down, w_ple, w_ple_gate, g_ple, g_final):
    b, seq, d = x_prompt.shape
    sb, st = x_sample.shape[0], x_sample.shape[1]
    assert st == 1 and b % SUBLANES == 0
    depth, n_groups, n_state = ssm_a_re.shape
    n_heads = att_bias.shape[1]
    hd = cache_k.shape[-1]
    aw = n_heads * hd
    sw = n_groups * ssm_b_re.shape[-1]
    sp = n_groups * n_state
    d_ff = w_up.shape[2]
    tm = min(PROMPT_TILE, seq)
    steps = min(SSM_STEPS, seq)
    n_sub = max(1, d_ff // FF_SUB)

    bbd, cbd, ab = _ssm_prep(ssm_a_re, ssm_a_im, ssm_log_dt, ssm_b_re, ssm_b_im, ssm_c_re, ssm_c_im)
    u2 = _suffix_matrix()
    pages_t = lambda c: c.transpose(0, 1, 3, 4, 2).reshape(depth, c.shape[1], aw, c.shape[2])
    cache_k, cache_v = pages_t(cache_k), pages_t(cache_v)

    def layer_weights(l, dt, kv_transposed):
        cast = lambda w: w[l].astype(dt)
        kv = (lambda w: cast(w).T) if kv_transposed else cast
        return {"w_q": cast(w_in[:, :, :aw]), "w_k": kv(w_in[:, :, aw:2 * aw]), "w_v": kv(w_in[:, :, 2 * aw:3 * aw]),
                "w_u": cast(w_in[:, :, 3 * aw:]), "w_out_a": cast(w_out[:, :aw]), "w_out_s": cast(w_out[:, aw:]),
                "w_up": cast(w_up), "w_down": cast(w_down), "w_ple": cast(w_ple),
                "w_ple_gate": cast(w_ple_gate), "w_glu": cast(w_glu), "bbd": cast(bbd), "cbd": cast(cbd),
                "g_att_out": g_att_out[l], "g_ssm_out": g_ssm_out[l], "g_ffn": g_ffn[l], "g_ple": g_ple[l]}

    hp = x_prompt.reshape(b * seq, d)
    hs = x_sample.reshape(sb, d)
    zero_state = jnp.zeros((b, 2 * sp), F32)
    outs = {name: [] for name in ("rp", "ip", "ks", "vs", "rs", "is")}
    kv_all = (jnp.zeros((depth, b, aw, seq), F32), jnp.zeros((depth, b, aw, seq), F32))
    for l in range(depth):
        final = l == depth - 1
        lw = layer_weights(l, BF16, True)
        q, k_all, v_all, u = _in_proj(hp, g_mix[l], lw, tm, prompt_batch=b, layer=l, depth=depth, kv_all=kv_all)
        kv_all = (k_all, v_all)
        att = _attn_prompt(q.reshape(b, seq, aw), k_all, v_all, l, att_bias[l], u2, hd)
        y, h_last = _ssm(u.reshape(seq * b, sw), lw["bbd"], lw["cbd"], ab[l], ssm_d[l].reshape(-1),
                         lw["w_glu"], zero_state, rows=b, steps=steps)
        hp = _tail(hp, att.reshape(b * seq, aw), y.reshape(seq, b * sw), p_prompt[l].reshape(b * seq, -1),
                   lw, g_final, tm, 1, n_sub, final, time_major_batch=b)
        outs["rp"].append(h_last[:, :sp].reshape(b, n_groups, n_state))
        outs["ip"].append(h_last[:, sp:].reshape(b, n_groups, n_state))

        lw = layer_weights(l, BF16, False)
        q, k, v, u = _in_proj(hs, g_mix[l], lw, sb)
        att = _attn_decode(q, att_bias[l], cache_k, cache_v, page_table, l, u2, hd)
        h0 = jnp.concatenate([state_ssm_re[l].reshape(sb, sp), state_ssm_im[l].reshape(sb, sp)], axis=1)
        y, h_last = _ssm(u, lw["bbd"], lw["cbd"], ab[l], ssm_d[l].reshape(-1), lw["w_glu"], h0,
                         rows=sb, steps=1)
        hs = _tail(hs, att, y, p_sample[l].reshape(sb, -1), lw, g_final, sb, 1, n_sub, final)
        outs["ks"].append(k.reshape(sb, 1, n_heads, hd))
        outs["vs"].append(v.reshape(sb, 1, n_heads, hd))
        outs["rs"].append(h_last[:, :sp].reshape(sb, n_groups, n_state))
        outs["is"].append(h_last[:, sp:].reshape(sb, n_groups, n_state))

    stack = lambda name: jnp.stack(outs[name])
    seq_major = lambda x: x.reshape(depth, b, n_heads, hd, seq).transpose(0, 1, 4, 2, 3)
    return (hp.reshape(b, seq, d), hs.reshape(sb, st, d),
            seq_major(kv_all[0]), seq_major(kv_all[1]), stack("rp"), stack("ip"),
            stack("ks"), stack("vs"), stack("rs"), stack("is"))
```

```python
import functools

import jax
import jax.numpy as jnp
from jax import lax
from jax.experimental import pallas as pl
from jax.experimental.pallas import tpu as pltpu

RMS_EPS = 1e-6
LANES = 128
SUBLANES = 8
VMEM_LIMIT = 56 * 1024 * 1024
BF16 = jnp.bfloat16
F32 = jnp.float32
_NT = (((1,), (1,)), ((), ()))
LOG2E = 1.4426950408889634
ATTN_GROUP = 4
ATTN_Q_ROWS = 256
ATTN_HEADS = 2
DECODE_PAGES = 16


def _mm(a, w):
    if w.dtype == BF16:
        return jnp.dot(a.astype(BF16), w, preferred_element_type=F32)
    return jnp.dot(a, w, preferred_element_type=F32, precision=lax.Precision.HIGHEST)


def _rms(x, g):
    return x * lax.rsqrt(jnp.mean(x * x, axis=-1, keepdims=True) + RMS_EPS) * g


def _const_spec(shape):
    nd = len(shape)
    return pl.BlockSpec(shape, lambda *_: (0,) * nd, pipeline_mode=pl.Buffered(1))


def _params(*sem):
    return pltpu.CompilerParams(dimension_semantics=sem, vmem_limit_bytes=VMEM_LIMIT)


def _ssm_prep_kernel(are_ref, aim_ref, ldt_ref, bre_ref, bim_ref, cre_ref, cim_ref,
                     bbd_ref, cbd_ref, ab_ref, *, c_shift, p_shift):
    a_re = are_ref[0]
    a_im = aim_ref[0]
    dt = jnp.exp(ldt_ref[0])
    mag = jnp.exp(a_re * dt)
    ab_re = mag * jnp.cos(a_im * dt)
    ab_im = mag * jnp.sin(a_im * dt)
    den = a_re * a_re + a_im * a_im
    f_re = ((ab_re - 1.0) * a_re + ab_im * a_im) / den
    f_im = (ab_im * a_re - (ab_re - 1.0) * a_im) / den
    cb, sp = bre_ref.shape[1], bre_ref.shape[2]
    ch0 = pl.program_id(1) * cb
    same_bc = ((lax.broadcasted_iota(jnp.int32, (cb, sp), 0) + ch0) >> c_shift) == \
              (lax.broadcasted_iota(jnp.int32, (cb, sp), 1) >> p_shift)
    br = bre_ref[0]
    bi = bim_ref[0]
    bbd_ref[0, :, :sp] = jnp.where(same_bc, f_re * br - f_im * bi, 0.0)
    bbd_ref[0, :, sp:] = jnp.where(same_bc, f_re * bi + f_im * br, 0.0)
    same_cb = (lax.broadcasted_iota(jnp.int32, (sp, cb), 0) >> p_shift) == \
              ((lax.broadcasted_iota(jnp.int32, (sp, cb), 1) + ch0) >> c_shift)
    cbd_ref[0, :sp, :] = jnp.where(same_cb, cre_ref[0], 0.0)
    cbd_ref[0, sp:, :] = jnp.where(same_cb, -cim_ref[0], 0.0)
    ab_ref[0, :, :sp] = jnp.broadcast_to(ab_re, (SUBLANES, sp))
    ab_ref[0, :, sp:] = jnp.broadcast_to(ab_im, (SUBLANES, sp))


def _ssm_prep(a_re, a_im, log_dt, b_re, b_im, c_re, c_im):
    depth, g, p = a_re.shape
    c = b_re.shape[-1]
    sw, sp = g * c, g * p
    assert c & (c - 1) == 0 and p & (p - 1) == 0 and sw % LANES == 0
    flat = lambda x: x.reshape(depth, 1, sp)
    ldt = jnp.repeat(log_dt, p, axis=1).reshape(depth, 1, sp)
    tile_b = lambda b: jnp.tile(b.transpose(0, 1, 3, 2).reshape(depth, sw, p), (1, 1, g))
    tile_c = lambda x: jnp.tile(x.transpose(0, 3, 1, 2).reshape(depth, p, sw), (1, g, 1))
    vec = pl.BlockSpec((1, 1, sp), lambda l, j: (l, 0, 0))
    bspec = pl.BlockSpec((1, LANES, sp), lambda l, j: (l, j, 0))
    cspec = pl.BlockSpec((1, sp, LANES), lambda l, j: (l, 0, j))
    return pl.pallas_call(
        functools.partial(_ssm_prep_kernel, c_shift=c.bit_length() - 1, p_shift=p.bit_length() - 1),
        grid=(depth, sw // LANES),
        in_specs=[vec, vec, vec, bspec, bspec, cspec, cspec],
        out_specs=[pl.BlockSpec((1, LANES, 2 * sp), lambda l, j: (l, j, 0)),
                   pl.BlockSpec((1, 2 * sp, LANES), lambda l, j: (l, 0, j)),
                   pl.BlockSpec((1, SUBLANES, 2 * sp), lambda l, j: (l, 0, 0))],
        out_shape=[jax.ShapeDtypeStruct((depth, sw, 2 * sp), F32),
                   jax.ShapeDtypeStruct((depth, 2 * sp, sw), F32),
                   jax.ShapeDtypeStruct((depth, SUBLANES, 2 * sp), F32)],
        compiler_params=_params("arbitrary", "arbitrary"),
        name="ssm_prep",
    )(flat(a_re), flat(a_im), ldt, tile_b(b_re), tile_b(b_im), tile_c(c_re), tile_c(c_im))


def _in_proj_kernel(x_ref, g_ref, wq_ref, wk_ref, wv_ref, wu_ref, *rest, kv_transposed):
    q_ref, k_ref, v_ref, u_ref = rest[-4:]
    xn = _rms(x_ref[...], g_ref[...])
    if wq_ref.dtype == BF16:
        xn = xn.astype(BF16)
    q_ref[...] = _mm(xn, wq_ref[...])
    u_ref[...] = _mm(xn, wu_ref[...])
    if kv_transposed:
        k_ref[0, 0] = _mm(xn, wk_ref[...]).T
        v_ref[0, 0] = _mm(xn, wv_ref[...]).T
    else:
        k_ref[...] = _mm(xn, wk_ref[...])
        v_ref[...] = _mm(xn, wv_ref[...])


def _in_proj(x, g, lw, tm, prompt_batch=None, layer=None, depth=None, kv_all=None):
    t, d = x.shape
    aw, sw = lw["w_q"].shape[1], lw["w_u"].shape[1]
    nt = t // tm
    row = pl.BlockSpec((tm, aw), lambda i: (i, 0))
    extra_in, extra_specs, aliases = [], [], {}
    if prompt_batch is None:
        kv_spec, kv_shape = row, (t, aw)
        u_spec, u_shape = pl.BlockSpec((tm, sw), lambda i: (i, 0)), (t, sw)
    else:
        nlt = nt // prompt_batch
        kv_spec = pl.BlockSpec((1, 1, aw, tm), lambda i: (layer, i // nlt, 0, i % nlt))
        kv_shape = (depth, prompt_batch, aw, t // prompt_batch)
        u_spec = pl.BlockSpec((tm, sw), lambda i: (i % nlt, i // nlt))
        u_shape = (t // prompt_batch, prompt_batch * sw)
        extra_in = list(kv_all)
        extra_specs = [pl.BlockSpec(memory_space=pl.ANY)] * 2
        aliases = {6: 1, 7: 2}
    return pl.pallas_call(
        functools.partial(_in_proj_kernel, kv_transposed=prompt_batch is not None),
        grid=(nt,),
        in_specs=[pl.BlockSpec((tm, d), lambda i: (i, 0)), _const_spec((1, d)), _const_spec(lw["w_q"].shape),
                  _const_spec(lw["w_k"].shape), _const_spec(lw["w_v"].shape), _const_spec(lw["w_u"].shape)]
                 + extra_specs,
        out_specs=[row, kv_spec, kv_spec, u_spec],
        out_shape=[jax.ShapeDtypeStruct((t, aw), F32), jax.ShapeDtypeStruct(kv_shape, F32),
                   jax.ShapeDtypeStruct(kv_shape, F32), jax.ShapeDtypeStruct(u_shape, F32)],
        input_output_aliases=aliases,
        compiler_params=_params("parallel"),
        name="in_proj",
    )(x, g.reshape(1, d), lw["w_q"], lw["w_k"], lw["w_v"], lw["w_u"], *extra_in)


def _stick_terms(z2):
    sp = jnp.maximum(z2, 0.0) + jnp.log2(1.0 + jnp.exp2(-jnp.abs(z2)))
    return sp, z2 - sp


def _suffix_sums(sp, u2_ref):
    return jnp.dot(sp.astype(BF16), u2_ref[...], preferred_element_type=F32)


def _suffix_matrix():
    j = lax.broadcasted_iota(jnp.int32, (LANES, 2 * LANES), 0)
    s = lax.broadcasted_iota(jnp.int32, (LANES, 2 * LANES), 1)
    return jnp.where((s >= LANES) | (j > s), 1.0, 0.0).astype(BF16)


def _attn_prompt_kernel(bias_ref, q_ref, k_ref, v_ref, u2_ref, o_ref, kb_ref, vb_ref, acc_ref, car_ref,
                        qm_ref, spb_ref, ls_ref, wb_ref, *, hd):
    nh, tq = acc_ref.shape[0], acc_ref.shape[1]
    qb = tq // LANES
    seq, lw = q_ref.shape[1], q_ref.shape[2]
    sup = kb_ref.shape[2] // LANES
    tk = sup * LANES
    sup_shift = sup.bit_length() - 1
    hp = pl.program_id(1)
    ones_rows = jnp.where(lax.broadcasted_iota(jnp.int32, (lw, tk), 0) < 2, 1.0, 0.0).astype(BF16)
    for s in range(seq // tk):
        cols = slice(s * tk, (s + 1) * tk)
        kb_ref[s] = jnp.concatenate([k_ref[0, 0, :, cols].astype(BF16), ones_rows], axis=0)
        vb_ref[s] = v_ref[0, 0, :, cols].T.astype(BF16)
    lane_q = lax.broadcasted_iota(jnp.int32, (tq, lw), 1)
    head_of_lane = lax.shift_right_logical(lane_q, hd.bit_length() - 1)
    first_of_pair = lax.broadcasted_iota(jnp.int32, (tq, LANES), 1) < hd
    causal = lax.broadcasted_iota(jnp.int32, (tq, tq), 1) < lax.broadcasted_iota(jnp.int32, (tq, tq), 0)
    scale = hd ** -0.5 * LOG2E
    heads = range(nh)
    bias_lanes = []
    for h in heads:
        b2 = jnp.full((tq, lw), bias_ref[nh * hp + h] * LOG2E, F32)
        b_hi = b2.astype(BF16).astype(F32)
        bias_lanes.append(jnp.where(lane_q == 0, b_hi, jnp.where(lane_q == 1, b2 - b_hi, 0.0)).astype(BF16))
    pair_lanes = lambda h: slice((h // 2) * LANES, (h // 2 + 1) * LANES)

    own = lambda nb, k: causal[:, (k - (nb - qb)) * LANES:(k - (nb - qb) + 1) * LANES]

    def scores_matmul(task):
        i, s, nb, diag = task
        if diag:
            q = q_ref[0, i * tq:(i + 1) * tq, :] * scale
            for h in heads:
                qm_ref[h] = jnp.concatenate([jnp.where(head_of_lane == h, q, 0.0).astype(BF16), bias_lanes[h]], axis=1)
        return [jnp.dot(qm_ref[h], kb_ref[s, :, :nb * LANES], preferred_element_type=F32) for h in heads]

    def scores_finish(task, slot, z):
        i, s, nb, diag = task
        for h in heads:
            sp, ls = _stick_terms(z[h])
            spk = [sp[:, k * LANES:(k + 1) * LANES] for k in range(nb)]
            if diag:
                for k in range(nb - qb, nb):
                    spk[k] = jnp.where(own(nb, k), spk[k], 0.0)
            spb_ref[slot, h, :nb * tq, :] = jnp.concatenate(spk, axis=0).astype(BF16)
            ls_ref[slot, h, :, :nb * LANES] = ls

    def weights_matmul(task, slot):
        i, s, nb, diag = task
        return [jnp.dot(spb_ref[slot, h, :nb * tq, :], u2_ref[...], preferred_element_type=F32) for h in heads]

    def weights_finish(task, slot, c):
        i, s, nb, diag = task
        for h in heads:
            carry = None if diag else car_ref[h]
            wh = [None] * nb
            for k in reversed(range(nb)):
                rows = slice(k * tq, (k + 1) * tq)
                t = ls_ref[slot, h, :, k * LANES:(k + 1) * LANES] - c[h][rows, :LANES]
                if carry is not None:
                    t = t - carry
                e = jnp.exp2(t)
                if diag and k >= nb - qb:
                    e = jnp.where(own(nb, k), e, 0.0)
                wh[k] = e.astype(BF16)
                tot = c[h][rows, LANES:]
                carry = tot if carry is None else carry + tot
            car_ref[h] = carry
            wb_ref[slot, h, :, :nb * LANES] = jnp.concatenate(wh, axis=1)

    def values(task, slot, last):
        i, s, nb, diag = task
        pv = [jnp.dot(wb_ref[slot, h, :, :nb * LANES], vb_ref[s, :nb * LANES, pair_lanes(h)],
                      preferred_element_type=F32) for h in heads]
        acc = [pv[h] if diag else acc_ref[h] + pv[h] for h in heads]
        if last:
            o_ref[0, i * tq:(i + 1) * tq, :] = jnp.concatenate(
                [jnp.where(first_of_pair, acc[h], acc[h + 1]) for h in range(0, nh, 2)], axis=1)
        else:
            for h in heads:
                acc_ref[h] = acc[h]

    tasks = []
    for i in range(seq // tq):
        last_block = i * qb + (qb - 1)
        s_own = last_block >> sup_shift
        tasks.append((i, s_own, (last_block & (sup - 1)) + 1, True))
        tasks.extend((i, s, sup, False) for s in range(s_own - 1, -1, -1))
    closes_step = [n + 1 == len(tasks) or tasks[n + 1][3] for n in range(len(tasks))]

    for n in range(len(tasks) + 2):
        z = scores_matmul(tasks[n]) if n < len(tasks) else None
        c = weights_matmul(tasks[n - 1], (n - 1) % 2) if 1 <= n <= len(tasks) else None
        if n >= 2:
            values(tasks[n - 2], n % 2, closes_step[n - 2])
        if z is not None:
            scores_finish(tasks[n], n % 2, z)
        if c is not None:
            weights_finish(tasks[n - 1], (n - 1) % 2, c)


def _attn_prompt(q, k_t, v_t, layer, bias, u2, hd):
    b, seq, aw = q.shape
    assert 2 * hd == LANES and seq % LANES == 0 and aw % LANES == 0
    sup = max(s for s in (1, 2, ATTN_GROUP) if seq % (s * LANES) == 0 and s <= ATTN_GROUP)
    tk = sup * LANES
    tq = ATTN_Q_ROWS if tk % ATTN_Q_ROWS == 0 else LANES
    lw = ATTN_HEADS * hd if aw % (ATTN_HEADS * hd) == 0 else LANES
    nh = lw // hd
    blk = pl.BlockSpec((1, seq, lw), lambda bi, hp: (bi, 0, hp))
    blk_t = pl.BlockSpec((1, 1, lw, seq), lambda bi, hp: (layer, bi, hp, 0))
    return pl.pallas_call(
        functools.partial(_attn_prompt_kernel, hd=hd),
        grid=(b, aw // lw),
        in_specs=[pl.BlockSpec(memory_space=pltpu.SMEM), blk, blk_t, blk_t, _const_spec((LANES, 2 * LANES))],
        out_specs=blk,
        out_shape=jax.ShapeDtypeStruct((b, seq, aw), F32),
        scratch_shapes=[pltpu.VMEM((seq // tk, 2 * lw, tk), BF16), pltpu.VMEM((seq // tk, tk, lw), BF16),
                        pltpu.VMEM((nh, tq, LANES), F32), pltpu.VMEM((nh, tq, LANES), F32),
                        pltpu.VMEM((nh, tq, 2 * lw), BF16), pltpu.VMEM((2, nh, sup * tq, LANES), BF16),
                        pltpu.VMEM((2, nh, tq, tk), F32), pltpu.VMEM((2, nh, tq, tk), BF16)],
        compiler_params=_params("parallel", "parallel"),
        name="attn_prompt",
    )(bias, q, k_t, v_t, u2)


def _attn_decode_kernel(pt_ref, q_ref, bias_ref, *refs, hd, npp):
    del pt_ref
    k_refs, v_refs = refs[:npp], refs[npp:2 * npp]
    u2_ref, o_ref, qbd_ref, acc_ref, car_ref = refs[2 * npp:]
    j = pl.program_id(1)
    rows, aw = qbd_ref.shape
    own = (lax.broadcasted_iota(jnp.int32, (rows, aw), 1) // hd) == lax.broadcasted_iota(jnp.int32, (rows, aw), 0)

    @pl.when(j == 0)
    def _():
        q = jnp.broadcast_to(q_ref[0] * (hd ** -0.5 * LOG2E), (rows, aw))
        qbd_ref[...] = jnp.where(own, q, 0.0).astype(BF16)
        acc_ref[...] = jnp.zeros_like(acc_ref)
        car_ref[...] = jnp.zeros_like(car_ref)

    qbd = qbd_ref[...]
    bias2 = bias_ref[...] * LOG2E
    terms = [_stick_terms(jnp.dot(qbd, k_refs[p][0, 0].astype(BF16), preferred_element_type=F32) + bias2)
             for p in range(npp)]
    c = _suffix_sums(jnp.concatenate([t[0] for t in terms], axis=0), u2_ref)
    carry = car_ref[...]
    pv = None
    for p in range(npp):
        r = slice(p * rows, (p + 1) * rows)
        w = jnp.exp2(terms[p][1] - c[r, :LANES] - carry).astype(BF16)
        carry = carry + c[r, LANES:]
        part = lax.dot_general(w, v_refs[p][0, 0].astype(BF16), _NT, preferred_element_type=F32)
        pv = part if pv is None else pv + part
    car_ref[...] = carry
    acc_ref[...] += pv

    @pl.when(j == pl.num_programs(1) - 1)
    def _():
        o_ref[0] = jnp.sum(jnp.where(own, acc_ref[...], 0.0), axis=0, keepdims=True)


def _attn_decode(q, bias, cache_k, cache_v, page_table, layer, u2, hd):
    s, aw = q.shape
    n_pages = page_table.shape[1]
    page = cache_k.shape[3]
    nh = aw // hd
    rows = 2 * SUBLANES
    assert page == LANES and nh <= rows
    bias_rows = jnp.broadcast_to(jnp.pad(bias, (0, rows - nh))[:, None], (rows, LANES))
    npp = max(n for n in range(1, DECODE_PAGES + 1) if n_pages % n == 0)

    def kv_spec(p):
        return pl.BlockSpec((1, 1, aw, page), lambda b, j, pt: (layer, pt[b, n_pages - 1 - (j * npp + p)], 0, 0))

    q_spec = pl.BlockSpec((1, 1, aw), lambda b, j, pt: (b, 0, 0))
    pages = [kv_spec(p) for p in range(npp)]
    out = pl.pallas_call(
        functools.partial(_attn_decode_kernel, hd=hd, npp=npp),
        grid_spec=pltpu.PrefetchScalarGridSpec(
            num_scalar_prefetch=1,
            grid=(s, n_pages // npp),
            in_specs=[q_spec, pl.BlockSpec((rows, LANES), lambda b, j, pt: (0, 0))] + pages + pages +
                     [pl.BlockSpec((LANES, 2 * LANES), lambda b, j, pt: (0, 0))],
            out_specs=q_spec,
            scratch_shapes=[pltpu.VMEM((rows, aw), BF16), pltpu.VMEM((rows, aw), F32),
                            pltpu.VMEM((rows, LANES), F32)],
        ),
        out_shape=jax.ShapeDtypeStruct((s, 1, aw), F32),
        compiler_params=_params("parallel", "arbitrary"),
        name="attn_decode",
    )(page_table, q.reshape(s, 1, aw), bias_rows, *([cache_k] * npp), *([cache_v] * npp), u2)
    return out.reshape(s, aw)


def _ssm_kernel(u_ref, bbd_ref, cbd_ref, ab_ref, d_ref, wglu_ref, h0_ref, y_ref, ht_ref, *scratch,
                rows, steps, cw):
    bu_refs, h_ref = scratch[:-1], scratch[-1]
    parts = len(bu_refs)
    psteps = steps // parts
    prow = psteps * rows
    c = pl.program_id(0)
    sp = h_ref.shape[1] // 2

    @pl.when(c == 0)
    def _():
        h_ref[...] = h0_ref[...]

    sw = u_ref.shape[1]
    halves = [(slice(k * (sw // 2), (k + 1) * (sw // 2)), k * (sp // 2), (k + 1) * (sp // 2)) for k in range(2)]
    for p in range(parts):
        u = u_ref[p * prow:(p + 1) * prow, :]
        for ch, lo, hi in halves:
            for off in (0, sp):
                bu_refs[p][:, off + lo:off + hi] = _mm(u[:, ch], bbd_ref[ch, off + lo:off + hi])

    for p in range(parts):
        bu_ref = bu_refs[p]
        for cc in range(sp // cw):
            re = slice(cc * cw, (cc + 1) * cw)
            im = slice(sp + cc * cw, sp + (cc + 1) * cw)
            a_r = jnp.broadcast_to(ab_ref[0:1, re], (rows, cw))
            a_i = jnp.broadcast_to(ab_ref[0:1, im], (rows, cw))
            h_r, h_i = h_ref[:, re], h_ref[:, im]
            for t in range(psteps):
                r = slice(t * rows, (t + 1) * rows)
                h_r, h_i = a_r * h_r - a_i * h_i + bu_ref[r, re], a_r * h_i + a_i * h_r + bu_ref[r, im]
                bu_ref[r, re] = h_r
                bu_ref[r, im] = h_i
            h_ref[:, re] = h_r
            h_ref[:, im] = h_i

        y = jnp.concatenate(
            [_mm(bu_ref[:, lo:hi], cbd_ref[lo:hi, ch]) + _mm(bu_ref[:, sp + lo:sp + hi], cbd_ref[sp + lo:sp + hi, ch])
             for ch, lo, hi in halves], axis=1) + d_ref[...] * u_ref[p * prow:(p + 1) * prow, :]
        y = jax.nn.gelu(y, approximate=True)
        y_ref[p * prow:(p + 1) * prow, :] = y * jax.nn.sigmoid(_mm(y, wglu_ref[...]))

    @pl.when(c == pl.num_programs(0) - 1)
    def _():
        ht_ref[...] = h_ref[...]


def _ssm(u, bbd, cbd, ab, d_skip, w_glu, h0, rows, steps):
    n, sw = u.shape
    sp2 = bbd.shape[1]
    chunk = rows * steps
    cw = 4 * LANES * SUBLANES // rows if rows <= 4 * SUBLANES else LANES
    cw = max(LANES, min(cw, sp2 // 2))
    parts = SSM_PARTS if steps % SSM_PARTS == 0 else 1
    blk = pl.BlockSpec((chunk, sw), lambda c: (c, 0))
    return pl.pallas_call(
        functools.partial(_ssm_kernel, rows=rows, steps=steps, cw=cw),
        grid=(n // chunk,),
        in_specs=[blk, _const_spec((sw, sp2)), _const_spec((sp2, sw)), _const_spec((SUBLANES, sp2)),
                  _const_spec((1, sw)), _const_spec((sw, sw)), _const_spec((rows, sp2))],
        out_specs=[blk, pl.BlockSpec((rows, sp2), lambda c: (0, 0))],
        out_shape=[jax.ShapeDtypeStruct((n, sw), F32), jax.ShapeDtypeStruct((rows, sp2), F32)],
        scratch_shapes=[pltpu.VMEM((chunk // parts, sp2), F32)] * parts + [pltpu.VMEM((rows, sp2), F32)],
        compiler_params=_params("arbitrary"),
        name="ssm",
    )(u, bbd, cbd, ab, d_skip.reshape(1, sw), w_glu, h0)


def _tail_kernel(h_ref, att_ref, ssm_ref, p_ref, ga_ref, gs_ref, woa_ref, wos_ref, gf_ref, wup_ref, wdn_ref,
                 gp_ref, wg_ref, wp_ref, gfin_ref, o_ref, acc_ref, xn_ref, *, n_sub, final):
    c = pl.program_id(1)

    @pl.when(c == 0)
    def _():
        na = _rms(att_ref[...], ga_ref[...])
        ns = _rms(ssm_ref[...], gs_ref[...])
        h1 = h_ref[...] + _mm(na, woa_ref[...]) + _mm(ns, wos_ref[...])
        acc_ref[...] = h1
        xn_ref[...] = _rms(h1, gf_ref[...]).astype(xn_ref.dtype)

    xn = xn_ref[...]
    sub = wup_ref.shape[1] // n_sub
    ffn = None
    for s in range(n_sub):
        f = _mm(xn, wup_ref[:, s * sub:(s + 1) * sub])
        part = _mm(jnp.square(jnp.maximum(f, 0.0)), wdn_ref[s * sub:(s + 1) * sub, :])
        ffn = part if ffn is None else ffn + part
    acc_ref[...] += ffn

    @pl.when(c == pl.num_programs(1) - 1)
    def _():
        h2 = acc_ref[...]
        gate = jax.nn.sigmoid(_mm(_rms(h2, gp_ref[...]), wg_ref[...]))
        h3 = h2 + _mm(p_ref[...], wp_ref[...]) * gate
        o_ref[...] = _rms(h3, gfin_ref[...]) if final else h3


def _tail(h, att, ssm, p, lw, g_final, tm, ff_chunks, n_sub, final, time_major_batch=None):
    t, d = h.shape
    aw, sw, pd = att.shape[1], lw["w_out_s"].shape[0], p.shape[1]
    d_ff = lw["w_up"].shape[1]
    ffc = d_ff // ff_chunks
    nt = t // tm
    wdt = lw["w_up"].dtype
    if time_major_batch is None:
        ssm_spec = pl.BlockSpec((tm, sw), lambda i, c: (i, 0))
    else:
        nlt = nt // time_major_batch
        ssm_spec = pl.BlockSpec((tm, sw), lambda i, c: (i % nlt, i // nlt))
    tok = lambda w: pl.BlockSpec((tm, w), lambda i, c: (i, 0))
    if ff_chunks == 1:
        up_spec, dn_spec = _const_spec((d, d_ff)), _const_spec((d_ff, d))
    else:
        up_spec = pl.BlockSpec((d, ffc), lambda i, c: (0, c))
        dn_spec = pl.BlockSpec((ffc, d), lambda i, c: (c, 0))
    vec = lambda g: g.reshape(1, -1)
    return pl.pallas_call(
        functools.partial(_tail_kernel, n_sub=n_sub, final=final),
        grid=(nt, ff_chunks),
        in_specs=[tok(d), tok(aw), ssm_spec, tok(pd),
                  _const_spec((1, aw)), _const_spec((1, sw)), _const_spec((aw, d)), _const_spec((sw, d)),
                  _const_spec((1, d)), up_spec, dn_spec,
                  _const_spec((1, d)), _const_spec((d, d)), _const_spec((pd, d)), _const_spec((1, d))],
        out_specs=tok(d),
        out_shape=jax.ShapeDtypeStruct((t, d), F32),
        scratch_shapes=[pltpu.VMEM((tm, d), F32), pltpu.VMEM((tm, d), wdt)],
        compiler_params=_params("parallel", "arbitrary"),
        name="tail",
    )(h, att, ssm, p, vec(lw["g_att_out"]), vec(lw["g_ssm_out"]), lw["w_out_a"], lw["w_out_s"],
      vec(lw["g_ffn"]), lw["w_up"], lw["w_down"], vec(lw["g_ple"]), lw["w_ple_gate"], lw["w_ple"], vec(g_final))


PROMPT_TILE = 512
SSM_STEPS = 64
SSM_PARTS = 2
FF_SUB = 1024


def kernel(x_prompt, x_sample, p_prompt, p_sample, cache_k, cache_v, state_ssm_re, state_ssm_im, page_table, w_in, w_out, g_mix, g_att_out, g_ssm_out, att_bias, ssm_a_re, ssm_a_im, ssm_log_dt, ssm_b_re, ssm_b_im, ssm_c_re, ssm_c_im, ssm_d, w_glu, g_ffn, w_up, w_down, w_ple, w_ple_gate, g_ple, g_final):
    b, seq, d = x_prompt.shape
    sb, st = x_sample.shape[0], x_sample.shape[1]
    assert st == 1 and b % SUBLANES == 0
    depth, n_groups, n_state = ssm_a_re.shape
    n_heads = att_bias.shape[1]
    hd = cache_k.shape[-1]
    aw = n_heads * hd
    sw = n_groups * ssm_b_re.shape[-1]
    sp = n_groups * n_state
    d_ff = w_up.shape[2]
    tm = min(PROMPT_TILE, seq)
    steps = min(SSM_STEPS, seq)
    n_sub = max(1, d_ff // FF_SUB)

    bbd, cbd, ab = _ssm_prep(ssm_a_re, ssm_a_im, ssm_log_dt, ssm_b_re, ssm_b_im, ssm_c_re, ssm_c_im)
    u2 = _suffix_matrix()
    pages_t = lambda c: c.transpose(0, 1, 3, 4, 2).reshape(depth, c.shape[1], aw, c.shape[2])
    cache_k, cache_v = pages_t(cache_k), pages_t(cache_v)

    def layer_weights(l, dt, kv_transposed):
        cast = lambda w: w[l].astype(dt)
        kv = cast
        return {"w_q": cast(w_in[:, :, :aw]), "w_k": kv(w_in[:, :, aw:2 * aw]), "w_v": kv(w_in[:, :, 2 * aw:3 * aw]),
                "w_u": cast(w_in[:, :, 3 * aw:]), "w_out_a": cast(w_out[:, :aw]), "w_out_s": cast(w_out[:, aw:]),
                "w_up": cast(w_up), "w_down": cast(w_down), "w_ple": cast(w_ple),
                "w_ple_gate": cast(w_ple_gate), "w_glu": cast(w_glu), "bbd": cast(bbd), "cbd": cast(cbd),
                "g_att_out": g_att_out[l], "g_ssm_out": g_ssm_out[l], "g_ffn": g_ffn[l], "g_ple": g_ple[l]}

    hp = x_prompt.reshape(b * seq, d)
    hs = x_sample.reshape(sb, d)
    zero_state = jnp.zeros((b, 2 * sp), F32)
    outs = {name: [] for name in ("rp", "ip", "ks", "vs", "rs", "is")}
    kv_all = (jnp.zeros((depth, b, aw, seq), F32), jnp.zeros((depth, b, aw, seq), F32))
    for l in range(depth):
        final = l == depth - 1
        lw = layer_weights(l, BF16, True)
        q, k_all, v_all, u = _in_proj(hp, g_mix[l], lw, tm, prompt_batch=b, layer=l, depth=depth, kv_all=kv_all)
        kv_all = (k_all, v_all)
        att = _attn_prompt(q.reshape(b, seq, aw), k_all, v_all, l, att_bias[l], u2, hd)
        y, h_last = _ssm(u.reshape(seq * b, sw), lw["bbd"], lw["cbd"], ab[l], ssm_d[l].reshape(-1),
                         lw["w_glu"], zero_state, rows=b, steps=steps)
        hp = _tail(hp, att.reshape(b * seq, aw), y.reshape(seq, b * sw), p_prompt[l].reshape(b * seq, -1),
                   lw, g_final, tm, 1, n_sub, final, time_major_batch=b)
        outs["rp"].append(h_last[:, :sp].reshape(b, n_groups, n_state))
        outs["ip"].append(h_last[:, sp:].reshape(b, n_groups, n_state))

        lw = layer_weights(l, BF16, False)
        q, k, v, u = _in_proj(hs, g_mix[l], lw, sb)
        att = _attn_decode(q, att_bias[l], cache_k, cache_v, page_table, l, u2, hd)
        h0 = jnp.concatenate([state_ssm_re[l].reshape(sb, sp), state_ssm_im[l].reshape(sb, sp)], axis=1)
        y, h_last = _ssm(u, lw["bbd"], lw["cbd"], ab[l], ssm_d[l].reshape(-1), lw["w_glu"], h0,
                         rows=sb, steps=1)
        hs = _tail(hs, att, y, p_sample[l].reshape(sb, -1), lw, g_final, sb, 1, n_sub, final)
        outs["ks"].append(k.reshape(sb, 1, n_heads, hd))
        outs["vs"].append(v.reshape(sb, 1, n_heads, hd))
        outs["rs"].append(h_last[:, :sp].reshape(sb, n_groups, n_state))
        outs["is"].append(h_last[:, sp:].reshape(sb, n_groups, n_state))

    stack = lambda name: jnp.stack(outs[name])
    seq_major = lambda x: x.reshape(depth, b, n_heads, hd, seq).transpose(0, 1, 4, 2, 3)
    return (hp.reshape(b, seq, d), hs.reshape(sb, st, d),
            seq_major(kv_all[0]), seq_major(kv_all[1]), stack("rp"), stack("ip"),
            stack("ks"), stack("vs"), stack("rs"), stack("is"))
```
